```python
import jax, jax.numpy as jnp
from jax import lax
import numpy as np

D_MODEL = 2048
BATCH = 4
SEQ = 2048
DEPTH = 2
DEC_BATCH = 128
DEC_SEQ = 4
PAST_LEN = 16384
PAGE_SIZE = 128

D_MIX = D_MODEL
HG_W = D_MIX // 4
ML_W = D_MIX // 2
RG_W = D_MIX - HG_W - ML_W
HG_H = 4
HG_DK = 128
HG_DV = HG_W // HG_H
LB_FLOOR = 1e-30
ML_H = 4
ML_DV = ML_W // ML_H
ML_DK = ML_DV // 2
GATE_CAP = 15.0
RG_BLOCKS = 8
RG_BW = RG_W // RG_BLOCKS
CONV_W = 4
RG_C = 8.0
N_MEM = 256
XA_H = 4
XA_DH = 128
MOE_G = 4
MOE_E = 4
MOE_NE = MOE_G * MOE_E
MOE_F = D_MODEL // 4
MOE_K = 2
CHUNK = 64
EPS = 1e-6
NEG = -1e30
IN_SIZES = (HG_H * HG_DK, HG_H * HG_DK, HG_W, HG_W,
            ML_H * ML_DK, ML_H * ML_DK, ML_W, ML_W, ML_H, ML_H,
            RG_W, RG_W)
N_IN = sum(IN_SIZES)

kernel_name = 'hymba_style_hgrn2_mlstm_rglru_hmoe_step'


def rms_norm(x, w):
    xf = x.astype(jnp.float32)
    y = xf * lax.rsqrt(jnp.mean(xf * xf, axis=-1, keepdims=True) + EPS)
    return (y * w.astype(jnp.float32)).astype(x.dtype)


def _head_rms(o, w):
    n_h, d_h = o.shape[-2:]
    o = o * lax.rsqrt(jnp.mean(o * o, axis=-1, keepdims=True) + EPS)
    return o * w.astype(jnp.float32).reshape(n_h, d_h)


def _chunk_len(t):
    return CHUNK if t % CHUNK == 0 else t


def _to_chunks(a, c):
    b, t = a.shape[:2]
    return jnp.moveaxis(a.reshape((b, t // c, c) + a.shape[2:]), 1, 0)


def _from_chunks(a):
    nc, b, c = a.shape[:3]
    return jnp.moveaxis(a, 0, 1).reshape((b, nc * c) + a.shape[3:])


def hgrn2_mix(q, f_pre, v, g, lb, s0, norm_w):
    f32 = jnp.float32
    b_sz, t = q.shape[:2]
    lbh = lb.astype(f32).reshape(HG_H, HG_DK)
    zf = f_pre.astype(f32)
    log_lb = jnp.log(jnp.maximum(lbh, LB_FLOOR))
    log_f = jnp.logaddexp(log_lb, jnp.log1p(-lbh) + jax.nn.log_sigmoid(zf))
    k = (1.0 - lbh) * jax.nn.sigmoid(-zf)
    c = _chunk_len(t)
    mask = jnp.tril(jnp.ones((c, c), dtype=bool))

    def step(s, inp):
        qc, kc, lfc, vc = inp
        b = jnp.cumsum(lfc, axis=1)
        o_inter = jnp.einsum('bthk,bhkv->bthv', qc * jnp.exp(b), s)
        diff = b[:, :, None] - b[:, None]
        decay = jnp.exp(jnp.where(mask[None, :, :, None, None], diff, NEG))
        scores = jnp.einsum('bthk,bshk,btshk->bhts', qc, kc, decay)
        o_intra = jnp.einsum('bhts,bshv->bthv', scores, vc)
        b_last = b[:, -1]
        k_dec = kc * jnp.exp(b_last[:, None] - b)
        s = jnp.exp(b_last)[..., None] * s + jnp.einsum('bshk,bshv->bhkv', k_dec, vc)
        return s, o_inter + o_intra

    xs = tuple(_to_chunks(a, c) for a in (q.astype(f32), k, log_f, v.astype(f32)))
    s_last, o = lax.scan(step, s0.astype(f32), xs)
    o = _from_chunks(o)
    y = _head_rms(o, norm_w) * jax.nn.silu(g.astype(f32))
    return y.reshape(b_sz, t, HG_H * HG_DV), s_last


def mlstm_mix(q, k, v, o_pre, i_pre, f_pre, c0, n0, m0, norm_w):
    f32 = jnp.float32
    b_sz, t = q.shape[:2]
    q = q.astype(f32)
    k = k.astype(f32) * (ML_DK ** -0.5)
    v = v.astype(f32)
    log_i = GATE_CAP * jnp.tanh(i_pre / GATE_CAP)
    log_f = jax.nn.log_sigmoid(GATE_CAP * jnp.tanh(f_pre / GATE_CAP))
    c = _chunk_len(t)
    mask = jnp.tril(jnp.ones((c, c), dtype=bool))

    def step(carry, inp):
        cm, n, m = carry
        qc, kc, vc, lic, lfc = inp
        b = jnp.cumsum(lfc, axis=1)
        d = b[:, :, None] - b[:, None] + lic[:, None]
        d = jnp.where(mask[None, :, :, None], d, NEG)
        inter = b + m[:, None]
        m_t = jnp.maximum(inter, jnp.max(d, axis=2))
        w_inter = jnp.exp(inter - m_t)
        w_intra = jnp.exp(d - m_t[:, :, None]) * jnp.einsum('bthk,bshk->btsh', qc, kc)
        num = (w_inter[..., None] * jnp.einsum('bthk,bhkv->bthv', qc, cm)
               + jnp.einsum('btsh,bshv->bthv', w_intra, vc))
        den = w_inter * jnp.einsum('bthk,bhk->bth', qc, n) + jnp.sum(w_intra, axis=2)
        h = num / jnp.maximum(jnp.abs(den), jnp.exp(-m_t))[..., None]
        b_last, m_last = b[:, -1], m_t[:, -1]
        w_prev = jnp.exp(b_last + m - m_last)
        w_new = jnp.exp(b_last[:, None] - b + lic - m_last[:, None])
        cm = w_prev[..., None, None] * cm + jnp.einsum('bsh,bshk,bshv->bhkv', w_new, kc, vc)
        n = w_prev[..., None] * n + jnp.einsum('bsh,bshk->bhk', w_new, kc)
        return (cm, n, m_last), h

    xs = tuple(_to_chunks(a, c) for a in (q, k, v, log_i, log_f))
    init = (c0.astype(f32), n0.astype(f32), m0.astype(f32))
    (c_last, n_last, m_last), h = lax.scan(step, init, xs)
    h = _from_chunks(h)
    y = _head_rms(h, norm_w) * jax.nn.sigmoid(o_pre.astype(f32))
    return y.reshape(b_sz, t, ML_H * ML_DV), c_last, n_last, m_last


def _lin_combine(left, right):
    a_l, b_l = left
    a_r, b_r = right
    return a_l * a_r, a_r * b_l + b_r


def rglru_mix(xb, gb, buf, h0, conv_w, conv_b, wa, ba, wx, bx, lam):
    f32 = jnp.float32
    b_sz, t, w = xb.shape
    xp = jnp.concatenate([buf.astype(f32), xb.astype(f32)], axis=1)
    u = conv_b.astype(f32)
    for j in range(CONV_W):
        u = u + xp[:, j:j + t] * conv_w[j].astype(f32)
    new_buf = xp[:, t:]
    ub = u.reshape(b_sz, t, RG_BLOCKS, RG_BW)
    r = jax.nn.sigmoid(jnp.einsum('btnk,nkj->btnj', ub, wa.astype(f32)).reshape(b_sz, t, w)
                       + ba.astype(f32))
    i = jax.nn.sigmoid(jnp.einsum('btnk,nkj->btnj', ub, wx.astype(f32)).reshape(b_sz, t, w)
                       + bx.astype(f32))
    log_a = -RG_C * r * jax.nn.softplus(-lam.astype(f32))
    a = jnp.exp(log_a)
    b_in = jnp.sqrt(-jnp.expm1(2.0 * log_a)) * (i * u)
    b_in = b_in.at[:, 0].add(a[:, 0] * h0.astype(f32))
    _, h = lax.associative_scan(_lin_combine, (a, b_in), axis=1)
    y = h * jax.nn.gelu(gb.astype(f32))
    return y, h[:, -1], new_buf


def memory_kv(mem, norm_w, wk, wv):
    b_sz, n_mem, _ = mem.shape
    m = rms_norm(mem, norm_w)
    k = (m @ wk).reshape(b_sz, n_mem, XA_H, XA_DH)
    v = (m @ wv).reshape(b_sz, n_mem, XA_H, XA_DH)
    return k, v


def cross_attend(h, mk, mv, wq, wo):
    b_sz, t, _ = h.shape
    q = (h @ wq).reshape(b_sz, t, XA_H, XA_DH)
    s = jnp.einsum('bthd,bmhd->bhtm', q, mk).astype(jnp.float32) * (XA_DH ** -0.5)
    p = jax.nn.softmax(s, axis=-1).astype(mv.dtype)
    o = jnp.einsum('bhtm,bmhd->bthd', p, mv).reshape(b_sz, t, XA_H * XA_DH)
    return (o @ wo).astype(h.dtype)


def hier_moe(h, wg_grp, bg_grp, wg_exp, bg_exp, w1, w3, w2):
    f32 = jnp.float32
    b_sz, t, d = h.shape
    xt = h.reshape(b_sz * t, d)
    g_prob = jax.nn.softmax((xt @ wg_grp).astype(f32) + bg_grp.astype(f32), axis=-1)
    g_w, g_idx = lax.top_k(g_prob, 1)
    e_logit = ((xt @ wg_exp).astype(f32) + bg_exp.astype(f32)).reshape(-1, MOE_G, MOE_E)
    e_sel = jnp.take_along_axis(e_logit, g_idx[:, :, None], axis=1)[:, 0]
    e_val, e_idx = lax.top_k(e_sel, MOE_K)
    w_k = jax.nn.softmax(e_val, axis=-1) * g_w
    expert = g_idx * MOE_E + e_idx
    combine = jnp.einsum('nk,nke->ne', w_k,
                         jax.nn.one_hot(expert, MOE_NE, dtype=f32)).astype(xt.dtype)
    hg = jnp.einsum('nd,edf->nef', xt, w1)
    hu = jnp.einsum('nd,edf->nef', xt, w3)
    act = jax.nn.silu(hg) * hu * combine[:, :, None]
    y = jnp.einsum('nef,efd->nd', act, w2)
    return y.reshape(b_sz, t, d).astype(h.dtype)


def decoder_layer(x, mem_k, mem_v, states, lb, lp):
    f32 = jnp.float32
    b_sz, t, _ = x.shape
    z = rms_norm(x, lp['norm_mix_w']) @ lp['w_in']
    split_at = np.cumsum(IN_SIZES)[:-1].tolist()
    hq, hf, hi, hgt, mq, mk, mv, mo, mi, mf, rx, rgt = jnp.split(z, split_at, axis=-1)
    s_hg, c_ml, n_ml, m_ml, h_rg, buf_rg = states
    y_a, s_hg = hgrn2_mix(hq.reshape(b_sz, t, HG_H, HG_DK), hf.reshape(b_sz, t, HG_H, HG_DK),
                          hi.reshape(b_sz, t, HG_H, HG_DV), hgt.reshape(b_sz, t, HG_H, HG_DV),
                          lb, s_hg, lp['hg_norm_w'])
    i_pre = mi.astype(f32) + lp['ml_if_b'][:ML_H].astype(f32)
    f_pre = mf.astype(f32) + lp['ml_if_b'][ML_H:].astype(f32)
    y_b, c_ml, n_ml, m_ml = mlstm_mix(mq.reshape(b_sz, t, ML_H, ML_DK), mk.reshape(b_sz, t, ML_H, ML_DK),
                                      mv.reshape(b_sz, t, ML_H, ML_DV), mo.reshape(b_sz, t, ML_H, ML_DV),
                                      i_pre, f_pre, c_ml, n_ml, m_ml, lp['ml_norm_w'])
    y_c, h_rg, buf_rg = rglru_mix(rx, rgt, buf_rg, h_rg, lp['rg_conv_w'], lp['rg_conv_b'],
                                  lp['rg_wa'], lp['rg_ba'], lp['rg_wx'], lp['rg_bx'], lp['rg_lambda'])
    mix = jnp.concatenate([y_a, y_b, y_c], axis=-1).astype(x.dtype)
    x = x + mix @ lp['w_out']
    x = x + cross_attend(rms_norm(x, lp['norm_xa_w']), mem_k, mem_v, lp['xa_wq'], lp['xa_wo'])
    x = x + hier_moe(rms_norm(x, lp['norm_ffn_w']), lp['moe_wg_grp'], lp['moe_bg_grp'],
                     lp['moe_wg_exp'], lp['moe_bg_exp'], lp['moe_w1'], lp['moe_w3'], lp['moe_w2'])
    return x, (s_hg, c_ml, n_ml, m_ml, h_rg, buf_rg)


def setup_inputs(seed: int = 0) -> dict:
    key = jax.random.key(seed)
    ks = iter(jax.random.split(key, 64))
    f32 = jnp.float32

    def nrm(shape, scale):
        return scale * jax.random.normal(next(ks), shape, f32)

    def gain(shape):
        return 1.0 + nrm(shape, 0.02)

    u = jax.random.uniform(next(ks), (DEPTH, RG_W), f32, 0.9, 0.999)
    a_base = u ** (1.0 / RG_C)
    rg_lambda = jnp.log(a_base) - jnp.log1p(-a_base)
    ml_if_b = jnp.concatenate([nrm((DEPTH, ML_H), 0.1), 3.0 + nrm((DEPTH, ML_H), 0.5)], axis=-1)
    return {
        'x_prompt': nrm((BATCH, SEQ, D_MODEL), 1.0),
        'x_sample': nrm((DEC_BATCH, DEC_SEQ, D_MODEL), 1.0),
        'mem_prompt': nrm((BATCH, N_MEM, D_MODEL), 1.0),
        'cache_mem_k': nrm((DEPTH, DEC_BATCH, N_MEM, XA_H, XA_DH), 1.0),
        'cache_mem_v': nrm((DEPTH, DEC_BATCH, N_MEM, XA_H, XA_DH), 1.0),
        'state_hg': nrm((DEPTH, DEC_BATCH, HG_H, HG_DK, HG_DV), 0.5),
        'state_ml_c': nrm((DEPTH, DEC_BATCH, ML_H, ML_DK, ML_DV), 0.1),
        'state_ml_n': jnp.abs(nrm((DEPTH, DEC_BATCH, ML_H, ML_DK), 0.5)),
        'state_ml_m': nrm((DEPTH, DEC_BATCH, ML_H), 1.0),
        'state_rg_h': nrm((DEPTH, DEC_BATCH, RG_W), 0.5),
        'state_rg_conv': nrm((DEPTH, DEC_BATCH, CONV_W - 1, RG_W), 1.0),
        'norm_mix_w': gain((DEPTH, D_MODEL)),
        'w_in': nrm((DEPTH, D_MODEL, N_IN), D_MODEL ** -0.5),
        'hg_lb': nrm((DEPTH, HG_H * HG_DK), 0.5),
        'hg_norm_w': gain((DEPTH, HG_H * HG_DV)),
        'ml_if_b': ml_if_b,
        'ml_norm_w': gain((DEPTH, ML_H * ML_DV)),
        'rg_conv_w': nrm((DEPTH, CONV_W, RG_W), 0.5),
        'rg_conv_b': nrm((DEPTH, RG_W), 0.1),
        'rg_wa': nrm((DEPTH, RG_BLOCKS, RG_BW, RG_BW), RG_BW ** -0.5),
        'rg_ba': nrm((DEPTH, RG_W), 0.1),
        'rg_wx': nrm((DEPTH, RG_BLOCKS, RG_BW, RG_BW), RG_BW ** -0.5),
        'rg_bx': nrm((DEPTH, RG_W), 0.1),
        'rg_lambda': rg_lambda,
        'w_out': nrm((DEPTH, D_MIX, D_MODEL), D_MIX ** -0.5),
        'norm_xa_w': gain((DEPTH, D_MODEL)),
        'norm_mem_w': gain((DEPTH, D_MODEL)),
        'xa_wq': nrm((DEPTH, D_MODEL, XA_H * XA_DH), D_MODEL ** -0.5),
        'xa_wk': nrm((DEPTH, D_MODEL, XA_H * XA_DH), D_MODEL ** -0.5),
        'xa_wv': nrm((DEPTH, D_MODEL, XA_H * XA_DH), D_MODEL ** -0.5),
        'xa_wo': nrm((DEPTH, XA_H * XA_DH, D_MODEL), (XA_H * XA_DH) ** -0.5),
        'norm_ffn_w': gain((DEPTH, D_MODEL)),
        'moe_wg_grp': nrm((DEPTH, D_MODEL, MOE_G), D_MODEL ** -0.5),
        'moe_bg_grp': nrm((DEPTH, MOE_G), 0.01),
        'moe_wg_exp': nrm((DEPTH, D_MODEL, MOE_NE), D_MODEL ** -0.5),
        'moe_bg_exp': nrm((DEPTH, MOE_NE), 0.01),
        'moe_w1': nrm((DEPTH, MOE_NE, D_MODEL, MOE_F), D_MODEL ** -0.5),
        'moe_w3': nrm((DEPTH, MOE_NE, D_MODEL, MOE_F), D_MODEL ** -0.5),
        'moe_w2': nrm((DEPTH, MOE_NE, MOE_F, D_MODEL), MOE_F ** -0.5),
        'norm_final_w': gain((D_MODEL,)),
    }


def reference(x_prompt, x_sample, mem_prompt, cache_mem_k, cache_mem_v, state_hg, state_ml_c,
              state_ml_n, state_ml_m, state_rg_h, state_rg_conv, norm_mix_w, w_in, hg_lb,
              hg_norm_w, ml_if_b, ml_norm_w, rg_conv_w, rg_conv_b, rg_wa, rg_ba, rg_wx, rg_bx,
              rg_lambda, w_out, norm_xa_w, norm_mem_w, xa_wq, xa_wk, xa_wv, xa_wo, norm_ffn_w,
              moe_wg_grp, moe_bg_grp, moe_wg_exp, moe_bg_exp, moe_w1, moe_w3, moe_w2,
              norm_final_w):
    f32 = jnp.float32
    p_lb = jax.nn.softmax(hg_lb.astype(f32), axis=0)
    lb_all = jnp.cumsum(p_lb, axis=0) - p_lb[0]
    bp = x_prompt.shape[0]
    zero_states = (jnp.zeros((bp, HG_H, HG_DK, HG_DV), f32),
                   jnp.zeros((bp, ML_H, ML_DK, ML_DV), f32),
                   jnp.zeros((bp, ML_H, ML_DK), f32),
                   jnp.zeros((bp, ML_H), f32),
                   jnp.zeros((bp, RG_W), f32),
                   jnp.zeros((bp, CONV_W - 1, RG_W), f32))
    hp, hs = x_prompt, x_sample
    mem_k_list, mem_v_list, st_p_list, st_s_list = [], [], [], []
    for l in range(DEPTH):
        lp = {'norm_mix_w': norm_mix_w[l], 'w_in': w_in[l], 'hg_norm_w': hg_norm_w[l],
              'ml_if_b': ml_if_b[l], 'ml_norm_w': ml_norm_w[l], 'rg_conv_w': rg_conv_w[l],
              'rg_conv_b': rg_conv_b[l], 'rg_wa': rg_wa[l], 'rg_ba': rg_ba[l], 'rg_wx': rg_wx[l],
              'rg_bx': rg_bx[l], 'rg_lambda': rg_lambda[l], 'w_out': w_out[l],
              'norm_xa_w': norm_xa_w[l], 'xa_wq': xa_wq[l], 'xa_wo': xa_wo[l],
              'norm_ffn_w': norm_ffn_w[l], 'moe_wg_grp': moe_wg_grp[l], 'moe_bg_grp': moe_bg_grp[l],
              'moe_wg_exp': moe_wg_exp[l], 'moe_bg_exp': moe_bg_exp[l], 'moe_w1': moe_w1[l],
              'moe_w3': moe_w3[l], 'moe_w2': moe_w2[l]}
        mk_p, mv_p = memory_kv(mem_prompt, norm_mem_w[l], xa_wk[l], xa_wv[l])
        hp, st_p = decoder_layer(hp, mk_p, mv_p, zero_states, lb_all[l], lp)
        st_s_in = (state_hg[l], state_ml_c[l], state_ml_n[l], state_ml_m[l],
                   state_rg_h[l], state_rg_conv[l])
        hs, st_s = decoder_layer(hs, cache_mem_k[l], cache_mem_v[l], st_s_in, lb_all[l], lp)
        mem_k_list.append(mk_p)
        mem_v_list.append(mv_p)
        st_p_list.append(st_p)
        st_s_list.append(st_s)
    y_prompt = rms_norm(hp, norm_final_w)
    y_sample = rms_norm(hs, norm_final_w)
    mem_k_prompt = jnp.stack(mem_k_list)
    mem_v_prompt = jnp.stack(mem_v_list)
    hg_p, mlc_p, mln_p, mlm_p, rgh_p, rgc_p = [jnp.stack(s) for s in zip(*st_p_list)]
    hg_s, mlc_s, mln_s, mlm_s, rgh_s, rgc_s = [jnp.stack(s) for s in zip(*st_s_list)]
    return (y_prompt, y_sample, mem_k_prompt, mem_v_prompt, hg_p, mlc_p, mln_p, mlm_p, rgh_p, rgc_p,
            hg_s, mlc_s, mln_s, mlm_s, rgh_s, rgc_s)
```

```python
import functools

import jax
import jax.numpy as jnp
import numpy as np
from jax import lax
from jax.experimental import pallas as pl
from jax.experimental.pallas import tpu as pltpu

F32 = jnp.float32
BF16 = jnp.bfloat16

D_MODEL = 2048
BATCH = 4
SEQ = 2048
DEPTH = 2
DEC_BATCH = 128
DEC_SEQ = 4
HG_H = 4
HG_DK = 128
HG_DV = 128
HG_W = HG_H * HG_DV
LB_FLOOR = 1e-30
ML_H = 4
ML_DV = 256
ML_DK = 128
ML_W = ML_H * ML_DV
GATE_CAP = 15.0
RG_W = 512
RG_BLOCKS = 8
RG_BW = RG_W // RG_BLOCKS
CONV_W = 4
RG_C = 8.0
N_MEM = 256
XA_H = 4
XA_DH = 128
XA_W = XA_H * XA_DH
MOE_G = 4
MOE_E = 4
MOE_NE = MOE_G * MOE_E
MOE_F = D_MODEL // 4
MOE_K = 2
EPS = 1e-6
NEG = -1e30

N_PROMPT = BATCH * SEQ
N_SAMPLE = DEC_BATCH * DEC_SEQ
N_TOK = N_PROMPT + N_SAMPLE
Z_MAIN = 2 * HG_H * HG_DK + 2 * HG_W + 2 * ML_H * ML_DK + 2 * ML_W
N_GATE = 2 * ML_H

LANES = 128
SUBLANES = 8
VMEM_LIMIT = 56 * 1024 * 1024

TM = 1088
TN = 512
CHUNK = 64
SUB = 16
TB_MIX = 512
SB_GRP = 8
TQ = 512
TM_MOE = 256


def _cparams(sem):
    return pltpu.CompilerParams(dimension_semantics=sem, vmem_limit_bytes=VMEM_LIMIT)


def _sigmoid(x):
    return 1.0 / (1.0 + jnp.exp(-x))


def _softplus(x):
    return jnp.maximum(x, 0.0) + jnp.log1p(jnp.exp(-jnp.abs(x)))


def _log_sigmoid(x):
    return -_softplus(-x)


def _expm1(y):
    e = jnp.exp(y)
    em1 = e - 1.0
    return jnp.where(e == 1.0, y, jnp.where(em1 == -1.0, -1.0, em1 * y / jnp.log(e)))


def _dot(a, b):
    return jnp.dot(a, b, preferred_element_type=F32)


def _dot_nt(a, b):
    return lax.dot_general(a, b, (((1,), (1,)), ((), ())), preferred_element_type=F32)


def _dot_tn(a, b):
    return lax.dot_general(a, b, (((0,), (0,)), ((), ())), preferred_element_type=F32)


def _rmsnorm_body(x_ref, w_ref, o_ref):
    x = x_ref[...]
    ms = jnp.mean(x * x, axis=-1, keepdims=True)
    o_ref[...] = ((x * lax.rsqrt(ms + EPS)) * w_ref[...]).astype(o_ref.dtype)


def _rmsnorm(x, w, out_dtype, tm):
    n, d = x.shape
    return pl.pallas_call(
        _rmsnorm_body,
        grid=(n // tm,),
        in_specs=[pl.BlockSpec((tm, d), lambda i: (i, 0)),
                  pl.BlockSpec((1, d), lambda i: (0, 0))],
        out_specs=pl.BlockSpec((tm, d), lambda i: (i, 0)),
        out_shape=jax.ShapeDtypeStruct((n, d), out_dtype),
        compiler_params=_cparams(("parallel",)),
        name="rmsnorm",
    )(x, w.reshape(1, d))


def _mm_body(*refs, has_res):
    if has_res:
        a_ref, w_ref, r_ref, o_ref, wb_ref = refs
    else:
        a_ref, w_ref, o_ref, wb_ref = refs

    @pl.when(pl.program_id(1) == 0)
    def _():
        wb_ref[...] = w_ref[...].astype(BF16)

    acc = _dot(a_ref[...], wb_ref[...])
    if has_res:
        acc = acc + r_ref[...]
    o_ref[...] = acc.astype(o_ref.dtype)


def _matmul(a, w, *, n_out=None, col0=0, tm, tn, res=None, out_dtype=F32, name="matmul"):
    m, k = a.shape
    n_out = w.shape[1] if n_out is None else n_out
    assert m % tm == 0 and n_out % tn == 0 and col0 % tn == 0
    cb = col0 // tn
    in_specs = [pl.BlockSpec((tm, k), lambda j, i: (i, 0)),
                pl.BlockSpec((k, tn), lambda j, i: (0, j + cb))]
    args = [a, w]
    if res is not None:
        in_specs.append(pl.BlockSpec((tm, tn), lambda j, i: (i, j)))
        args.append(res)
    return pl.pallas_call(
        functools.partial(_mm_body, has_res=res is not None),
        grid=(n_out // tn, m // tm),
        in_specs=in_specs,
        out_specs=pl.BlockSpec((tm, tn), lambda j, i: (i, j)),
        out_shape=jax.ShapeDtypeStruct((m, n_out), out_dtype),
        scratch_shapes=[pltpu.VMEM((k, tn), BF16)],
        compiler_params=_cparams(("arbitrary", "arbitrary")),
        name=name,
    )(*args)


def _hgrn2_chunk(q, zf, v, lbh, s, valid):
    c = q.shape[0]
    sb = min(SUB, c)
    log_lb = jnp.log(jnp.maximum(lbh, LB_FLOOR))
    lsig = _log_sigmoid(zf)
    a0 = log_lb
    a1 = jnp.log1p(-lbh) + lsig
    log_f = jnp.maximum(a0, a1) + jnp.log1p(jnp.exp(-jnp.abs(a0 - a1)))
    k = (1.0 - lbh) * _sigmoid(-zf)
    if valid is not None:
        log_f = jnp.where(valid, log_f, 0.0)
        k = jnp.where(valid, k, 0.0)
    row = lax.broadcasted_iota(jnp.int32, (c, c), 0)
    col = lax.broadcasted_iota(jnp.int32, (c, c), 1)
    tril = (col <= row).astype(F32)
    b = jnp.dot(tril, log_f, precision=lax.Precision.HIGHEST, preferred_element_type=F32)
    o_inter = _dot((q * jnp.exp(b)).astype(BF16), s.astype(BF16))
    ones = jnp.ones((LANES, LANES), BF16)
    srow = lax.broadcasted_iota(jnp.int32, (sb, LANES), 0)
    outs = []
    for i in range(c // sb):
        r0 = i * sb
        bi, qi, ki, vi = (x[r0:r0 + sb] for x in (b, q, k, v))
        ps = []
        for si in range(sb):
            diff = jnp.where(srow >= si, bi - bi[si:si + 1], NEG)
            ps.append((qi * ki[si:si + 1]) * jnp.exp(diff))
        r = _dot(jnp.concatenate(ps, axis=0).astype(BF16), ones)
        od = r[0:sb] * vi[0:1]
        for si in range(1, sb):
            od = od + r[si * sb:(si + 1) * sb] * vi[si:si + 1]
        if i > 0:
            ref = bi[0:1]
            qp = qi * jnp.exp(bi - ref)
            kp = k[:r0] * jnp.exp(ref - b[:r0])
            sc = _dot_nt(qp.astype(BF16), kp.astype(BF16))
            od = od + _dot(sc.astype(BF16), v[:r0].astype(BF16))
        outs.append(o_inter[r0:r0 + sb] + od)
    o = outs[0] if len(outs) == 1 else jnp.concatenate(outs, axis=0)
    b_last = b[c - 1:c]
    kdec = k * jnp.exp(b_last - b)
    dec_kv = jnp.transpose(jnp.broadcast_to(jnp.exp(b_last), (LANES, LANES)))
    s_new = dec_kv * s + _dot_tn(kdec.astype(BF16), v.astype(BF16))
    return o, s_new


def _hg_out(o, g, nw):
    o = o * lax.rsqrt(jnp.mean(o * o, axis=-1, keepdims=True) + EPS)
    return (o * nw) * (g * _sigmoid(g))


def _hg_prompt_body(q_ref, f_ref, v_ref, g_ref, lb_ref, nw_ref, y_ref, s_ref):
    @pl.when(pl.program_id(2) == 0)
    def _():
        s_ref[...] = jnp.zeros_like(s_ref)

    lbh = lb_ref[...]
    nw = nw_ref[...]

    def chunk(ci, carry):
        r = pl.multiple_of(ci * CHUNK, CHUNK)
        rows = pl.ds(r, CHUNK)
        o, s_new = _hgrn2_chunk(q_ref[rows, :], f_ref[rows, :], v_ref[rows, :], lbh, s_ref[0, 0], None)
        s_ref[0, 0] = s_new
        y_ref[rows, :] = _hg_out(o, g_ref[rows, :], nw)
        return carry

    lax.fori_loop(0, TB_MIX // CHUNK, chunk, 0)


def _pair_rows(x, par):
    return pltpu.roll(x, 4, 0) if par else x


def _hg_sample_body(q_ref, f_ref, v_ref, g_ref, lb_ref, nw_ref, s_in_ref, y_ref, s_ref):
    lbh = lb_ref[...]
    nw = nw_ref[...]
    rid = lax.broadcasted_iota(jnp.int32, (SUBLANES, 1), 0)
    valid = rid < DEC_SEQ
    for pr in range(SB_GRP // 2):
        rows = slice(SUBLANES * pr, SUBLANES * (pr + 1))
        q8, f8, v8 = q_ref[rows, :], f_ref[rows, :], v_ref[rows, :]
        os_ = []
        for par in range(2):
            j = 2 * pr + par
            o, s_new = _hgrn2_chunk(_pair_rows(q8, par), _pair_rows(f8, par), _pair_rows(v8, par),
                                    lbh, s_in_ref[j, 0], valid)
            s_ref[j, 0] = s_new
            os_.append(_pair_rows(o, par))
        o8 = jnp.where(valid, os_[0], os_[1])
        y_ref[rows, :] = _hg_out(o8, g_ref[rows, :], nw)


def _hgrn2(z_main, lb, nw, s_sample):
    lb2 = lb.reshape(1, HG_H * HG_DK)
    nw2 = nw.reshape(1, HG_W)
    nt = SEQ // TB_MIX
    col = lambda base: (lambda b, h, c: (b * nt + c, base + h))
    y_p, s_p = pl.pallas_call(
        _hg_prompt_body,
        grid=(BATCH, HG_H, nt),
        in_specs=[pl.BlockSpec((TB_MIX, LANES), col(0)),
                  pl.BlockSpec((TB_MIX, LANES), col(HG_H)),
                  pl.BlockSpec((TB_MIX, LANES), col(2 * HG_H)),
                  pl.BlockSpec((TB_MIX, LANES), col(3 * HG_H)),
                  pl.BlockSpec((1, LANES), lambda b, h, c: (0, h)),
                  pl.BlockSpec((1, LANES), lambda b, h, c: (0, h))],
        out_specs=[pl.BlockSpec((TB_MIX, LANES), lambda b, h, c: (b * nt + c, h)),
                   pl.BlockSpec((1, 1, HG_DK, HG_DV), lambda b, h, c: (b, h, 0, 0))],
        out_shape=[jax.ShapeDtypeStruct((N_PROMPT, HG_W), F32),
                   jax.ShapeDtypeStruct((BATCH, HG_H, HG_DK, HG_DV), F32)],
        compiler_params=_cparams(("parallel", "parallel", "arbitrary")),
        name="hgrn2_prompt",
    )(z_main, z_main, z_main, z_main, lb2, nw2)

    rows = SB_GRP * DEC_SEQ
    rb0 = N_PROMPT // rows
    scol = lambda base: (lambda g, h: (rb0 + g, base + h))
    y_s, s_s = pl.pallas_call(
        _hg_sample_body,
        grid=(DEC_BATCH // SB_GRP, HG_H),
        in_specs=[pl.BlockSpec((rows, LANES), scol(0)),
                  pl.BlockSpec((rows, LANES), scol(HG_H)),
                  pl.BlockSpec((rows, LANES), scol(2 * HG_H)),
                  pl.BlockSpec((rows, LANES), scol(3 * HG_H)),
                  pl.BlockSpec((1, LANES), lambda g, h: (0, h)),
                  pl.BlockSpec((1, LANES), lambda g, h: (0, h)),
                  pl.BlockSpec((SB_GRP, 1, HG_DK, HG_DV), lambda g, h: (g, h, 0, 0))],
        out_specs=[pl.BlockSpec((rows, LANES), lambda g, h: (g, h)),
                   pl.BlockSpec((SB_GRP, 1, HG_DK, HG_DV), lambda g, h: (g, h, 0, 0))],
        out_shape=[jax.ShapeDtypeStruct((N_SAMPLE, HG_W), F32),
                   jax.ShapeDtypeStruct((DEC_BATCH, HG_H, HG_DK, HG_DV), F32)],
        compiler_params=_cparams(("parallel", "parallel")),
        name="hgrn2_sample",
    )(z_main, z_main, z_main, z_main, lb2, nw2, s_sample)
    return y_p, s_p, y_s, s_s


def _mlstm_chunk(q, k, v, i_pre, f_pre, cm, n, m, valid):
    c = q.shape[0]
    k = k * (ML_DK ** -0.5)
    li = GATE_CAP * jnp.tanh(i_pre / GATE_CAP)
    lf = _log_sigmoid(GATE_CAP * jnp.tanh(f_pre / GATE_CAP))
    if valid is not None:
        li = jnp.where(valid, li, NEG)
        lf = jnp.where(valid, lf, 0.0)
        k = jnp.where(valid, k, 0.0)
    row = lax.broadcasted_iota(jnp.int32, (c, c), 0)
    col = lax.broadcasted_iota(jnp.int32, (c, c), 1)
    eye = row == col
    causal = col <= row
    lf_row = jnp.sum(jnp.where(eye, lf, 0.0), axis=0, keepdims=True)
    li_row = jnp.sum(jnp.where(eye, li, 0.0), axis=0, keepdims=True)
    b_col = jnp.sum(jnp.where(causal, lf_row, 0.0), axis=1, keepdims=True)
    b_row = jnp.sum(jnp.where(row <= col, lf, 0.0), axis=0, keepdims=True)
    d = jnp.where(causal, b_col - b_row + li_row, NEG)
    inter = b_col + m
    m_t = jnp.maximum(inter, jnp.max(d, axis=1, keepdims=True))
    w_inter = jnp.exp(inter - m_t)
    qb = q.astype(BF16)
    w_intra = jnp.exp(d - m_t) * _dot_nt(qb, k.astype(BF16))
    num = w_inter * _dot(qb, cm.astype(BF16)) + _dot(w_intra.astype(BF16), v.astype(BF16))
    den = w_inter * jnp.sum(q * n, axis=1, keepdims=True) + jnp.sum(w_intra, axis=1, keepdims=True)
    h = num / jnp.maximum(jnp.abs(den), jnp.exp(-m_t))
    b_last = b_col[c - 1:c]
    m_last = m_t[c - 1:c]
    w_prev = jnp.exp(b_last + m - m_last)
    w_new = jnp.exp(b_last - b_col + li - m_last)
    kw = w_new * k
    cm_new = w_prev * cm + _dot_tn(kw.astype(BF16), v.astype(BF16))
    n_new = w_prev * n + jnp.sum(kw, axis=0, keepdims=True)
    return h, cm_new, n_new, m_last


def _ml_gates(gate, bias, h):
    lane = lax.broadcasted_iota(jnp.int32, gate.shape, 1)
    g = gate + bias
    i_pre = jnp.sum(jnp.where(lane == h, g, 0.0), axis=1, keepdims=True)
    f_pre = jnp.sum(jnp.where(lane == h + ML_H, g, 0.0), axis=1, keepdims=True)
    return i_pre, f_pre


def _ml_out(h, og, nw):
    h = h * lax.rsqrt(jnp.mean(h * h, axis=-1, keepdims=True) + EPS)
    return (h * nw) * _sigmoid(og)


def _ml_prompt_body(q_ref, k_ref, v_ref, o_ref, gate_ref, bias_ref, nw_ref, y_ref, c_ref, n_ref, m_ref):
    @pl.when(pl.program_id(2) == 0)
    def _():
        c_ref[...] = jnp.zeros_like(c_ref)
        n_ref[...] = jnp.zeros_like(n_ref)
        m_ref[...] = jnp.zeros_like(m_ref)

    hh = pl.program_id(1)
    bias = bias_ref[...]
    nw = nw_ref[...]

    def chunk(ci, carry):
        r = pl.multiple_of(ci * CHUNK, CHUNK)
        rows = pl.ds(r, CHUNK)
        i_pre, f_pre = _ml_gates(gate_ref[rows, :], bias, hh)
        h, cm, n, m = _mlstm_chunk(q_ref[rows, :], k_ref[rows, :], v_ref[rows, :], i_pre, f_pre,
                                   c_ref[0, 0], n_ref[0, 0], m_ref[0, 0][:, 0:1], None)
        c_ref[0, 0] = cm
        n_ref[0, 0] = n
        m_ref[0, 0] = jnp.broadcast_to(m, (1, LANES))
        y_ref[rows, :] = _ml_out(h, o_ref[rows, :], nw)
        return carry

    lax.fori_loop(0, TB_MIX // CHUNK, chunk, 0)


def _ml_sample_body(q_ref, k_ref, v_ref, o_ref, gate_ref, bias_ref, nw_ref, c_in_ref, n_in_ref, m_in_ref,
                    y_ref, c_ref, n_ref, m_ref):
    hh = pl.program_id(1)
    bias = bias_ref[...]
    nw = nw_ref[...]
    rid = lax.broadcasted_iota(jnp.int32, (SUBLANES, 1), 0)
    valid = rid < DEC_SEQ
    for pr in range(SB_GRP // 2):
        rows = slice(SUBLANES * pr, SUBLANES * (pr + 1))
        q8, k8, v8 = q_ref[rows, :], k_ref[rows, :], v_ref[rows, :]
        i8, f8 = _ml_gates(gate_ref[rows, :], bias, hh)
        hs = []
        for par in range(2):
            j = 2 * pr + par
            h, cm, n, m = _mlstm_chunk(_pair_rows(q8, par), _pair_rows(k8, par), _pair_rows(v8, par),
                                       _pair_rows(i8, par), _pair_rows(f8, par),
                                       c_in_ref[j, 0], n_in_ref[j, 0], m_in_ref[j, 0][:, 0:1], valid)
            c_ref[j, 0] = cm
            n_ref[j, 0] = n
            m_ref[j, 0] = jnp.broadcast_to(m, (1, LANES))
            hs.append(_pair_rows(h, par))
        h8 = jnp.where(valid, hs[0], hs[1])
        y_ref[rows, :] = _ml_out(h8, o_ref[rows, :], nw)


def _mlstm(z_main, z_gate, if_b, nw, c_s, n_s, m_s):
    bias = jnp.zeros((1, LANES), F32).at[0, :N_GATE].set(if_b)
    nw2 = nw.reshape(1, ML_W)
    nt = SEQ // TB_MIX
    qb = (2 * HG_H * HG_DK + 2 * HG_W) // LANES
    kb = qb + ML_H
    vb = (qb + 2 * ML_H) * LANES // ML_DV
    ob = vb + ML_H
    pcol = lambda base: (lambda b, h, c: (b * nt + c, base + h))
    st = lambda shape: pl.BlockSpec((1, 1) + shape, lambda b, h, c: (b, h, 0, 0))
    y_p, c_p, n_p, m_p = pl.pallas_call(
        _ml_prompt_body,
        grid=(BATCH, ML_H, nt),
        in_specs=[pl.BlockSpec((TB_MIX, ML_DK), pcol(qb)),
                  pl.BlockSpec((TB_MIX, ML_DK), pcol(kb)),
                  pl.BlockSpec((TB_MIX, ML_DV), pcol(vb)),
                  pl.BlockSpec((TB_MIX, ML_DV), pcol(ob)),
                  pl.BlockSpec((TB_MIX, LANES), lambda b, h, c: (b * nt + c, 0)),
                  pl.BlockSpec((1, LANES), lambda b, h, c: (0, 0)),
                  pl.BlockSpec((1, ML_DV), lambda b, h, c: (0, h))],
        out_specs=[pl.BlockSpec((TB_MIX, ML_DV), lambda b, h, c: (b * nt + c, h)),
                   st((ML_DK, ML_DV)), st((1, ML_DK)), st((1, LANES))],
        out_shape=[jax.ShapeDtypeStruct((N_PROMPT, ML_W), F32),
                   jax.ShapeDtypeStruct((BATCH, ML_H, ML_DK, ML_DV), F32),
                   jax.ShapeDtypeStruct((BATCH, ML_H, 1, ML_DK), F32),
                   jax.ShapeDtypeStruct((BATCH, ML_H, 1, LANES), F32)],
        compiler_params=_cparams(("parallel", "parallel", "arbitrary")),
        name="mlstm_prompt",
    )(z_main, z_main, z_main, z_main, z_gate, bias, nw2)

    rows = SB_GRP * DEC_SEQ
    rb0 = N_PROMPT // rows
    scol = lambda base: (lambda g, h: (rb0 + g, base + h))
    sst = lambda shape: pl.BlockSpec((SB_GRP, 1) + shape, lambda g, h: (g, h, 0, 0))
    n_in = n_s.reshape(DEC_BATCH, ML_H, 1, ML_DK)
    m_in = jnp.broadcast_to(m_s[:, :, None, None], (DEC_BATCH, ML_H, 1, LANES))
    y_s, c_o, n_o, m_o = pl.pallas_call(
        _ml_sample_body,
        grid=(DEC_BATCH // SB_GRP, ML_H),
        in_specs=[pl.BlockSpec((rows, ML_DK), scol(qb)),
                  pl.BlockSpec((rows, ML_DK), scol(kb)),
                  pl.BlockSpec((rows, ML_DV), scol(vb)),
                  pl.BlockSpec((rows, ML_DV), scol(ob)),
                  pl.BlockSpec((rows, LANES), lambda g, h: (rb0 + g, 0)),
                  pl.BlockSpec((1, LANES), lambda g, h: (0, 0)),
                  pl.BlockSpec((1, ML_DV), lambda g, h: (0, h)),
                  sst((ML_DK, ML_DV)), sst((1, ML_DK)), sst((1, LANES))],
        out_specs=[pl.BlockSpec((rows, ML_DV), lambda g, h: (g, h)),
                   sst((ML_DK, ML_DV)), sst((1, ML_DK)), sst((1, LANES))],
        out_shape=[jax.ShapeDtypeStruct((N_SAMPLE, ML_W), F32),
                   jax.ShapeDtypeStruct((DEC_BATCH, ML_H, ML_DK, ML_DV), F32),
                   jax.ShapeDtypeStruct((DEC_BATCH, ML_H, 1, ML_DK), F32),
                   jax.ShapeDtypeStruct((DEC_BATCH, ML_H, 1, LANES), F32)],
        compiler_params=_cparams(("parallel", "parallel")),
        name="mlstm_sample",
    )(z_main, z_main, z_main, z_main, z_gate, bias, nw2, c_s, n_in, m_in)
    return (y_p, c_p, n_p[:, :, 0, :], m_p[:, :, 0, 0]), (y_s, c_o, n_o[:, :, 0, :], m_o[:, :, 0, 0])


def _gelu_tanh(x):
    return 0.5 * x * (1.0 + jnp.tanh(np.sqrt(2.0 / np.pi).astype(np.float32) * (x + 0.044715 * (x * x * x))))


def _rg_gates(u, wbd_ref, bias, lam):
    ri = _dot(u.astype(BF16), wbd_ref[...]) + bias
    r = _sigmoid(ri[:, :RG_W])
    i = _sigmoid(ri[:, RG_W:])
    log_a = -RG_C * r * _softplus(-lam)
    a = jnp.exp(log_a)
    b_in = jnp.sqrt(-_expm1(2.0 * log_a)) * (i * u)
    return a, b_in


def _rg_prompt_body(x_ref, g_ref, cw_ref, cb_ref, wbd_ref, bias_ref, lam_ref, y_ref, h_ref, buf_ref,
                    xp_ref, a_ref, b_ref):
    tb = x_ref.shape[0]

    @pl.when(pl.program_id(1) == 0)
    def _():
        xp_ref[0:SUBLANES, :] = jnp.zeros((SUBLANES, RG_W), F32)
        h_ref[...] = jnp.zeros_like(h_ref)

    x = x_ref[...]
    xp_ref[SUBLANES:, :] = x
    u = cb_ref[...] + x * cw_ref[CONV_W - 1:CONV_W, :]
    for j in range(CONV_W - 1):
        sh = CONV_W - 1 - j
        u = u + xp_ref[SUBLANES - sh:SUBLANES - sh + tb, :] * cw_ref[j:j + 1, :]
    a, b_in = _rg_gates(u, wbd_ref, bias_ref[...], lam_ref[...])
    a_ref[...] = a
    b_ref[...] = b_in
    rid = lax.broadcasted_iota(jnp.int32, (SUBLANES, RG_W), 0)

    def step(i, h):
        r = pl.multiple_of(i * SUBLANES, SUBLANES)
        a8 = a_ref[pl.ds(r, SUBLANES), :]
        b8 = b_ref[pl.ds(r, SUBLANES), :]
        h8 = jnp.zeros((SUBLANES, RG_W), F32)
        for rr in range(SUBLANES):
            h = a8[rr:rr + 1] * h + b8[rr:rr + 1]
            h8 = jnp.where(rid == rr, h, h8)
        b_ref[pl.ds(r, SUBLANES), :] = h8
        return h

    h_last = lax.fori_loop(0, tb // SUBLANES, step, h_ref[0])
    h_ref[0] = h_last
    y_ref[...] = b_ref[...] * _gelu_tanh(g_ref[...])
    buf_ref[0] = x[tb - (CONV_W - 1):, :]
    xp_ref[0:SUBLANES, :] = x[tb - SUBLANES:, :]


def _rg_sample_body(x_ref, g_ref, buf_in_ref, h_in_ref, cw_ref, cb_ref, wbd_ref, bias_ref, lam_ref,
                    y_ref, h_ref, buf_ref):
    xs = [buf_in_ref[j] for j in range(CONV_W - 1)] + [x_ref[t] for t in range(DEC_SEQ)]
    h = h_in_ref[...]
    for t in range(DEC_SEQ):
        u = cb_ref[...]
        for j in range(CONV_W):
            u = u + xs[t + j] * cw_ref[j:j + 1, :]
        a, b_in = _rg_gates(u, wbd_ref, bias_ref[...], lam_ref[...])
        h = a * h + b_in
        y_ref[t] = h * _gelu_tanh(g_ref[t])
    h_ref[...] = h
    for j in range(CONV_W - 1):
        buf_ref[j] = xs[DEC_SEQ + j]


def _block_diag(w):
    eye = jnp.eye(RG_BLOCKS, dtype=w.dtype)
    return (eye[:, None, :, None] * w[:, :, None, :]).reshape(RG_W, RG_W)


def _rglru(z_rg, lp, h_s, buf_s):
    wbd = jnp.concatenate([_block_diag(lp['rg_wa']), _block_diag(lp['rg_wx'])], axis=1).astype(BF16)
    bias = jnp.concatenate([lp['rg_ba'], lp['rg_bx']]).reshape(1, 2 * RG_W)
    cw = lp['rg_conv_w']
    cb = lp['rg_conv_b'].reshape(1, RG_W)
    lam = lp['rg_lambda'].reshape(1, RG_W)
    tb = TB_MIX
    nt = SEQ // tb
    full = lambda shape: pl.BlockSpec(shape, lambda *_: (0,) * len(shape))
    y_p, h_p, buf_p = pl.pallas_call(
        _rg_prompt_body,
        grid=(BATCH, nt),
        in_specs=[pl.BlockSpec((tb, RG_W), lambda b, c: (b * nt + c, 0)),
                  pl.BlockSpec((tb, RG_W), lambda b, c: (b * nt + c, 1)),
                  full((CONV_W, RG_W)), full((1, RG_W)), full((RG_W, 2 * RG_W)), full((1, 2 * RG_W)),
                  full((1, RG_W))],
        out_specs=[pl.BlockSpec((tb, RG_W), lambda b, c: (b * nt + c, 0)),
                   pl.BlockSpec((1, 1, RG_W), lambda b, c: (b, 0, 0)),
                   pl.BlockSpec((1, CONV_W - 1, RG_W), lambda b, c: (b, 0, 0))],
        out_shape=[jax.ShapeDtypeStruct((N_PROMPT, RG_W), F32),
                   jax.ShapeDtypeStruct((BATCH, 1, RG_W), F32),
                   jax.ShapeDtypeStruct((BATCH, CONV_W - 1, RG_W), F32)],
        scratch_shapes=[pltpu.VMEM((tb + SUBLANES, RG_W), F32),
                        pltpu.VMEM((tb, RG_W), F32),
                        pltpu.VMEM((tb, RG_W), F32)],
        compiler_params=_cparams(("parallel", "arbitrary")),
        name="rglru_prompt",
    )(z_rg, z_rg, cw, cb, wbd, bias, lam)

    zs = z_rg[N_PROMPT:].reshape(DEC_BATCH, DEC_SEQ, 2 * RG_W).transpose(1, 0, 2)
    y_s, h_o, buf_o = pl.pallas_call(
        _rg_sample_body,
        grid=(1,),
        in_specs=[pl.BlockSpec((DEC_SEQ, DEC_BATCH, RG_W), lambda i: (0, 0, 0)),
                  pl.BlockSpec((DEC_SEQ, DEC_BATCH, RG_W), lambda i: (0, 0, 1)),
                  full((CONV_W - 1, DEC_BATCH, RG_W)), full((DEC_BATCH, RG_W)),
                  full((CONV_W, RG_W)), full((1, RG_W)), full((RG_W, 2 * RG_W)), full((1, 2 * RG_W)),
                  full((1, RG_W))],
        out_specs=[full((DEC_SEQ, DEC_BATCH, RG_W)), full((DEC_BATCH, RG_W)),
                   full((CONV_W - 1, DEC_BATCH, RG_W))],
        out_shape=[jax.ShapeDtypeStruct((DEC_SEQ, DEC_BATCH, RG_W), F32),
                   jax.ShapeDtypeStruct((DEC_BATCH, RG_W), F32),
                   jax.ShapeDtypeStruct((CONV_W - 1, DEC_BATCH, RG_W), F32)],
        compiler_params=_cparams(("arbitrary",)),
        name="rglru_sample",
    )(zs, zs, buf_s.transpose(1, 0, 2), h_s, cw, cb, wbd, bias, lam)
    y_s = y_s.transpose(1, 0, 2).reshape(N_SAMPLE, RG_W)
    return (y_p, h_p[:, 0, :], buf_p), (y_s, h_o, buf_o.transpose(1, 0, 2))


def _attn_body(q_ref, k_ref, v_ref, o_ref, *, nb, tq):
    q = q_ref[...]
    k = k_ref[...].astype(BF16)
    v = v_ref[...].astype(BF16)
    s = _dot_nt(q, k) * (XA_DH ** -0.5)
    if nb > 1:
        rb = lax.broadcasted_iota(jnp.int32, s.shape, 0) // tq
        cb = lax.broadcasted_iota(jnp.int32, s.shape, 1) // N_MEM
        s = jnp.where(rb == cb, s, NEG)
    p = jnp.exp(s - jnp.max(s, axis=-1, keepdims=True))
    p = p / jnp.sum(p, axis=-1, keepdims=True)
    o_ref[...] = _dot(p.astype(BF16), v).astype(o_ref.dtype)


def _cross_attend(q, mk_p, mv_p, mk_s, mv_s):
    nt = SEQ // TQ
    o_p = pl.pallas_call(
        functools.partial(_attn_body, nb=1, tq=TQ),
        grid=(BATCH, XA_H, nt),
        in_specs=[pl.BlockSpec((TQ, XA_DH), lambda b, h, c: (b * nt + c, h)),
                  pl.BlockSpec((N_MEM, XA_DH), lambda b, h, c: (b, h)),
                  pl.BlockSpec((N_MEM, XA_DH), lambda b, h, c: (b, h))],
        out_specs=pl.BlockSpec((TQ, XA_DH), lambda b, h, c: (b * nt + c, h)),
        out_shape=jax.ShapeDtypeStruct((N_PROMPT, XA_W), BF16),
        compiler_params=_cparams(("parallel", "parallel", "parallel")),
        name="xattn_prompt",
    )(q, mk_p, mv_p)
    nb = 4
    rows = nb * DEC_SEQ
    rb0 = N_PROMPT // rows
    o_s = pl.pallas_call(
        functools.partial(_attn_body, nb=nb, tq=DEC_SEQ),
        grid=(DEC_BATCH // nb, XA_H),
        in_specs=[pl.BlockSpec((rows, XA_DH), lambda g, h: (rb0 + g, h)),
                  pl.BlockSpec((nb * N_MEM, XA_DH), lambda g, h: (g, h)),
                  pl.BlockSpec((nb * N_MEM, XA_DH), lambda g, h: (g, h))],
        out_specs=pl.BlockSpec((rows, XA_DH), lambda g, h: (g, h)),
        out_shape=jax.ShapeDtypeStruct((N_SAMPLE, XA_W), BF16),
        compiler_params=_cparams(("parallel", "parallel")),
        name="xattn_sample",
    )(q, mk_s, mv_s)
    return jnp.concatenate([o_p, o_s], axis=0)


def _router_body(x_ref, w_ref, wg_ref, bg_ref, xn_ref, lg_ref):
    x = x_ref[...]
    ms = jnp.mean(x * x, axis=-1, keepdims=True)
    xn = (x * lax.rsqrt(ms + EPS)) * w_ref[...]
    xn_ref[...] = xn.astype(BF16)
    lg_ref[...] = jnp.dot(xn, wg_ref[...], precision=lax.Precision.HIGHEST,
                          preferred_element_type=F32) + bg_ref[...]


def _router(x, norm_w, wg, bg, tm):
    n, d = x.shape
    return pl.pallas_call(
        _router_body,
        grid=(n // tm,),
        in_specs=[pl.BlockSpec((tm, d), lambda i: (i, 0)),
                  pl.BlockSpec((1, d), lambda i: (0, 0)),
                  pl.BlockSpec((d, LANES), lambda i: (0, 0)),
                  pl.BlockSpec((1, LANES), lambda i: (0, 0))],
        out_specs=[pl.BlockSpec((tm, d), lambda i: (i, 0)),
                   pl.BlockSpec((tm, LANES), lambda i: (i, 0))],
        out_shape=[jax.ShapeDtypeStruct((n, d), BF16),
                   jax.ShapeDtypeStruct((n, LANES), F32)],
        compiler_params=_cparams(("parallel",)),
        name="moe_router",
    )(x, norm_w.reshape(1, d), wg, bg)


def _expert_body(te_ref, x_ref, w1_ref, w3_ref, w2_ref, rw_ref, o_ref, w1b, w3b, w2b):
    t = pl.program_id(0)
    prev = te_ref[jnp.maximum(t - 1, 0)]

    @pl.when((t == 0) | (te_ref[t] != prev))
    def _():
        w1b[...] = w1_ref[0].astype(BF16)
        w3b[...] = w3_ref[0].astype(BF16)
        w2b[...] = w2_ref[0].astype(BF16)

    x = x_ref[...]
    hg = _dot(x, w1b[...])
    hu = _dot(x, w3b[...])
    act = (hg * _sigmoid(hg)) * hu * rw_ref[...]
    o_ref[...] = _dot(act.astype(BF16), w2b[...])


def _experts(tile_expert, xs, w1, w3, w2, row_w):
    r, d = xs.shape
    nt = r // TM_MOE
    grid_spec = pltpu.PrefetchScalarGridSpec(
        num_scalar_prefetch=1,
        grid=(nt,),
        in_specs=[pl.BlockSpec((TM_MOE, d), lambda t, te: (t, 0)),
                  pl.BlockSpec((1, d, MOE_F), lambda t, te: (te[t], 0, 0)),
                  pl.BlockSpec((1, d, MOE_F), lambda t, te: (te[t], 0, 0)),
                  pl.BlockSpec((1, MOE_F, d), lambda t, te: (te[t], 0, 0)),
                  pl.BlockSpec((TM_MOE, 1), lambda t, te: (t, 0))],
        out_specs=pl.BlockSpec((TM_MOE, d), lambda t, te: (t, 0)),
        scratch_shapes=[pltpu.VMEM((d, MOE_F), BF16), pltpu.VMEM((d, MOE_F), BF16),
                        pltpu.VMEM((MOE_F, d), BF16)],
    )
    return pl.pallas_call(
        _expert_body,
        grid_spec=grid_spec,
        out_shape=jax.ShapeDtypeStruct((r, d), F32),
        compiler_params=_cparams(("arbitrary",)),
        name="moe_experts",
    )(tile_expert, xs, w1, w3, w2, row_w)


MOE_ROWS = ((MOE_K * N_TOK + MOE_NE * (TM_MOE - 1)) // TM_MOE) * TM_MOE


def _route(logits):
    g_prob = jax.nn.softmax(logits[:, :MOE_G], axis=-1)
    g_w, g_idx = lax.top_k(g_prob, 1)
    e_logit = logits[:, MOE_G:MOE_G + MOE_NE].reshape(-1, MOE_G, MOE_E)
    e_sel = jnp.take_along_axis(e_logit, g_idx[:, :, None], axis=1)[:, 0]
    e_val, e_idx = lax.top_k(e_sel, MOE_K)
    w_k = jax.nn.softmax(e_val, axis=-1) * g_w
    return g_idx * MOE_E + e_idx, w_k


def _dispatch_plan(expert, w_k):
    n = expert.shape[0]
    flat_e = expert.reshape(-1)
    onehot = (flat_e[:, None] == jnp.arange(MOE_NE, dtype=flat_e.dtype)[None, :]).astype(jnp.int32)
    rank = jnp.take_along_axis(jnp.cumsum(onehot, axis=0) - onehot, flat_e[:, None], axis=1)[:, 0]
    counts = jnp.sum(onehot, axis=0)
    padded = ((counts + TM_MOE - 1) // TM_MOE) * TM_MOE
    ends = jnp.cumsum(padded)
    starts = ends - padded
    pos = starts[flat_e] + rank
    tok = jnp.arange(n * MOE_K, dtype=jnp.int32) // MOE_K
    row_src = jnp.zeros((MOE_ROWS,), jnp.int32).at[pos].set(tok)
    row_w = jnp.zeros((MOE_ROWS,), F32).at[pos].set(w_k.reshape(-1))
    tile_start = jnp.arange(MOE_ROWS // TM_MOE, dtype=jnp.int32) * TM_MOE
    tile_expert = jnp.minimum(jnp.searchsorted(ends, tile_start, side='right'), MOE_NE - 1).astype(jnp.int32)
    return row_src, row_w.reshape(MOE_ROWS, 1), tile_expert, pos.reshape(n, MOE_K)


def _hier_moe(x, lp):
    wg = jnp.zeros((D_MODEL, LANES), F32)
    wg = wg.at[:, :MOE_G].set(lp['moe_wg_grp']).at[:, MOE_G:MOE_G + MOE_NE].set(lp['moe_wg_exp'])
    bg = jnp.zeros((1, LANES), F32)
    bg = bg.at[0, :MOE_G].set(lp['moe_bg_grp']).at[0, MOE_G:MOE_G + MOE_NE].set(lp['moe_bg_exp'])
    xn, logits = _router(x, lp['norm_ffn_w'], wg, bg, TM // 2)
    expert, w_k = _route(logits)
    row_src, row_w, tile_expert, pos = _dispatch_plan(expert, w_k)
    xs = jnp.take(xn, row_src, axis=0)
    ys = _experts(tile_expert, xs, lp['moe_w1'], lp['moe_w3'], lp['moe_w2'], row_w)
    return x + jnp.take(ys, pos[:, 0], axis=0) + jnp.take(ys, pos[:, 1], axis=0)


def _decoder_layer(x, lp, lb, mem_p, cache_k, cache_v, st_s):
    s_hg, c_ml, n_ml, m_ml, h_rg, buf_rg = st_s
    xn = _rmsnorm(x, lp['norm_mix_w'], BF16, TM // 2)
    w_in = lp['w_in']
    z_main = _matmul(xn, w_in, n_out=Z_MAIN, tm=TM, tn=TN, name="w_in_main")
    w_gate = jnp.zeros((D_MODEL, LANES), F32).at[:, :N_GATE].set(w_in[:, Z_MAIN:Z_MAIN + N_GATE])
    z_gate = _matmul(xn, w_gate, tm=TM, tn=LANES, name="w_in_gate")
    z_rg = _matmul(xn, w_in[:, Z_MAIN + N_GATE:], tm=TM, tn=TN, name="w_in_rg")

    ya_p, hg_p, ya_s, hg_s = _hgrn2(z_main, lb, lp['hg_norm_w'], s_hg)
    (yb_p, mlc_p, mln_p, mlm_p), (yb_s, mlc_s, mln_s, mlm_s) = _mlstm(
        z_main, z_gate, lp['ml_if_b'], lp['ml_norm_w'], c_ml, n_ml, m_ml)
    (yc_p, rgh_p, rgc_p), (yc_s, rgh_s, rgc_s) = _rglru(z_rg, lp, h_rg, buf_rg)
    mix = jnp.concatenate([jnp.concatenate([ya_p, yb_p, yc_p], axis=1),
                           jnp.concatenate([ya_s, yb_s, yc_s], axis=1)], axis=0).astype(BF16)
    x = _matmul(mix, lp['w_out'], tm=TM, tn=TN, res=x, name="w_out")

    mn = _rmsnorm(mem_p, lp['norm_mem_w'], BF16, BATCH * N_MEM // 2)
    mk_p = _matmul(mn, lp['xa_wk'], tm=BATCH * N_MEM, tn=XA_W, name="mem_k")
    mv_p = _matmul(mn, lp['xa_wv'], tm=BATCH * N_MEM, tn=XA_W, name="mem_v")

    xn = _rmsnorm(x, lp['norm_xa_w'], BF16, TM // 2)
    q = _matmul(xn, lp['xa_wq'], tm=TM, tn=XA_W, out_dtype=BF16, name="xa_q")
    o = _cross_attend(q, mk_p, mv_p, cache_k.reshape(DEC_BATCH * N_MEM, XA_W),
                      cache_v.reshape(DEC_BATCH * N_MEM, XA_W))
    x = _matmul(o, lp['xa_wo'], tm=TM, tn=TN, res=x, name="xa_o")

    x = _hier_moe(x, lp)
    st_p = (hg_p, mlc_p, mln_p, mlm_p, rgh_p, rgc_p)
    st_s = (hg_s, mlc_s, mln_s, mlm_s, rgh_s, rgc_s)
    return x, mk_p, mv_p, st_p, st_s


def kernel(x_prompt, x_sample, mem_prompt, cache_mem_k, cache_mem_v, state_hg, state_ml_c, state_ml_n, state_ml_m, state_rg_h, state_rg_conv, norm_mix_w, w_in, hg_lb, hg_norm_w, ml_if_b, ml_norm_w, rg_conv_w, rg_conv_b, rg_wa, rg_ba, rg_wx, rg_bx, rg_lambda, w_out, norm_xa_w, norm_mem_w, xa_wq, xa_wk, xa_wv, xa_wo, norm_ffn_w, moe_wg_grp, moe_bg_grp, moe_wg_exp, moe_bg_exp, moe_w1, moe_w3, moe_w2, norm_final_w):
    p_lb = jax.nn.softmax(hg_lb.astype(F32), axis=0)
    lb_all = jnp.cumsum(p_lb, axis=0) - p_lb[0]
    x = jnp.concatenate([x_prompt.reshape(N_PROMPT, D_MODEL), x_sample.reshape(N_SAMPLE, D_MODEL)], axis=0)
    mem_p = mem_prompt.reshape(BATCH * N_MEM, D_MODEL)
    per_layer = dict(norm_mix_w=norm_mix_w, w_in=w_in, hg_norm_w=hg_norm_w, ml_if_b=ml_if_b,
                     ml_norm_w=ml_norm_w, rg_conv_w=rg_conv_w, rg_conv_b=rg_conv_b, rg_wa=rg_wa,
                     rg_ba=rg_ba, rg_wx=rg_wx, rg_bx=rg_bx, rg_lambda=rg_lambda, w_out=w_out,
                     norm_xa_w=norm_xa_w, norm_mem_w=norm_mem_w, xa_wq=xa_wq, xa_wk=xa_wk, xa_wv=xa_wv,
                     xa_wo=xa_wo, norm_ffn_w=norm_ffn_w, moe_wg_grp=moe_wg_grp, moe_bg_grp=moe_bg_grp,
                     moe_wg_exp=moe_wg_exp, moe_bg_exp=moe_bg_exp, moe_w1=moe_w1, moe_w3=moe_w3,
                     moe_w2=moe_w2)
    mem_k, mem_v, st_p, st_s = [], [], [], []
    for l in range(DEPTH):
        lp = {k: v[l] for k, v in per_layer.items()}
        st_in = (state_hg[l], state_ml_c[l], state_ml_n[l], state_ml_m[l], state_rg_h[l], state_rg_conv[l])
        x, mk_p, mv_p, sp, ss = _decoder_layer(x, lp, lb_all[l], mem_p, cache_mem_k[l], cache_mem_v[l], st_in)
        mem_k.append(mk_p.reshape(BATCH, N_MEM, XA_H, XA_DH))
        mem_v.append(mv_p.reshape(BATCH, N_MEM, XA_H, XA_DH))
        st_p.append(sp)
        st_s.append(ss)
    y = _rmsnorm(x, norm_final_w, F32, TM // 2)
    y_prompt = y[:N_PROMPT].reshape(BATCH, SEQ, D_MODEL)
    y_sample = y[N_PROMPT:].reshape(DEC_BATCH, DEC_SEQ, D_MODEL)
    outs_p = [jnp.stack(s) for s in zip(*st_p)]
    outs_s = [jnp.stack(s) for s in zip(*st_s)]
    return (y_prompt, y_sample, jnp.stack(mem_k), jnp.stack(mem_v), *outs_p, *outs_s)
```

```python
import functools

import jax
import jax.numpy as jnp
import numpy as np
from jax import lax
from jax.experimental import pallas as pl
from jax.experimental.pallas import tpu as pltpu

F32 = jnp.float32
BF16 = jnp.bfloat16

D_MODEL = 2048
BATCH = 4
SEQ = 2048
DEPTH = 2
DEC_BATCH = 128
DEC_SEQ = 4
HG_H = 4
HG_DK = 128
HG_DV = 128
HG_W = HG_H * HG_DV
LB_FLOOR = 1e-30
ML_H = 4
ML_DV = 256
ML_DK = 128
ML_W = ML_H * ML_DV
GATE_CAP = 15.0
RG_W = 512
RG_BLOCKS = 8
RG_BW = RG_W // RG_BLOCKS
CONV_W = 4
RG_C = 8.0
N_MEM = 256
XA_H = 4
XA_DH = 128
XA_W = XA_H * XA_DH
MOE_G = 4
MOE_E = 4
MOE_NE = MOE_G * MOE_E
MOE_F = D_MODEL // 4
MOE_K = 2
EPS = 1e-6
NEG = -1e30

N_PROMPT = BATCH * SEQ
N_SAMPLE = DEC_BATCH * DEC_SEQ
N_TOK = N_PROMPT + N_SAMPLE
Z_MAIN = 2 * HG_H * HG_DK + 2 * HG_W + 2 * ML_H * ML_DK + 2 * ML_W
N_GATE = 2 * ML_H

LANES = 128
SUBLANES = 8
VMEM_LIMIT = 56 * 1024 * 1024

TM = 1088
TM_SPLIT = 512
TN = 512
CHUNK = 64
SUB = 16
TB_MIX = 512
SB_GRP = 8
TQ = 512
TM_MOE = 256


def _cparams(sem):
    return pltpu.CompilerParams(dimension_semantics=sem, vmem_limit_bytes=VMEM_LIMIT)


def _sigmoid(x):
    return 1.0 / (1.0 + jnp.exp(-x))


def _softplus(x):
    return jnp.maximum(x, 0.0) + jnp.log1p(jnp.exp(-jnp.abs(x)))


def _log_sigmoid(x):
    return -_softplus(-x)


def _expm1(y):
    e = jnp.exp(y)
    em1 = e - 1.0
    return jnp.where(e == 1.0, y, jnp.where(em1 == -1.0, -1.0, em1 * y / jnp.log(e)))


def _dot(a, b):
    return jnp.dot(a, b, preferred_element_type=F32)


def _dot_nt(a, b):
    return lax.dot_general(a, b, (((1,), (1,)), ((), ())), preferred_element_type=F32)


def _dot_tn(a, b):
    return lax.dot_general(a, b, (((0,), (0,)), ((), ())), preferred_element_type=F32)


def _rmsnorm_body(x_ref, w_ref, o_ref):
    x = x_ref[...]
    ms = jnp.mean(x * x, axis=-1, keepdims=True)
    o_ref[...] = ((x * lax.rsqrt(ms + EPS)) * w_ref[...]).astype(o_ref.dtype)


def _rmsnorm(x, w, out_dtype, tm):
    n, d = x.shape
    return pl.pallas_call(
        _rmsnorm_body,
        grid=(n // tm,),
        in_specs=[pl.BlockSpec((tm, d), lambda i: (i, 0)),
                  pl.BlockSpec((1, d), lambda i: (0, 0))],
        out_specs=pl.BlockSpec((tm, d), lambda i: (i, 0)),
        out_shape=jax.ShapeDtypeStruct((n, d), out_dtype),
        compiler_params=_cparams(("parallel",)),
        name="rmsnorm",
    )(x, w.reshape(1, d))


def _mm_body(*refs, n_a, ks, has_res, split):
    a_refs = refs[:n_a]
    w_ref = refs[n_a]
    r_ref = refs[n_a + 1] if has_res else None
    o_ref, wb_ref = refs[-2], refs[-1]
    i = pl.program_id(1)

    @pl.when(i == 0)
    def _():
        wb_ref[...] = w_ref[...].astype(BF16)

    def emit(pieces):
        acc = None
        off = 0
        for a_ref, k in zip(pieces, ks):
            part = _dot(a_ref[...], wb_ref[off:off + k, :])
            acc = part if acc is None else acc + part
            off += k
        if has_res:
            acc = acc + r_ref[...]
        o_ref[...] = acc.astype(o_ref.dtype)

    if split:
        pl.when(i < N_PROMPT // TM_SPLIT)(lambda: emit(a_refs[0::2]))
        pl.when(i == N_PROMPT // TM_SPLIT)(lambda: emit(a_refs[1::2]))
    else:
        emit(a_refs)


def _matmul(a, w, layer=None, *, n_out=None, col0=0, tm, tn, res=None, out_dtype=F32, name="matmul"):
    split = isinstance(a, (list, tuple))
    if split:
        tm = TM_SPLIT
        m = N_TOK
        ks = tuple(p.shape[1] for p, _ in a)
        last_prompt = N_PROMPT // tm - 1
        a_args, a_specs = [], []
        for (p, s), kp in zip(a, ks):
            a_args += [p, s]
            a_specs += [pl.BlockSpec((tm, kp), lambda j, i: (jnp.minimum(i, last_prompt), 0)),
                        pl.BlockSpec((tm, kp), lambda j, i: (0, 0))]
    else:
        m = a.shape[0]
        ks = (a.shape[1],)
        a_args = [a]
        a_specs = [pl.BlockSpec((tm, ks[0]), lambda j, i: (i, 0))]
    k = sum(ks)
    n_out = w.shape[-1] if n_out is None else n_out
    assert m % tm == 0 and n_out % tn == 0 and col0 % tn == 0
    cb = col0 // tn
    if layer is None:
        w_spec = pl.BlockSpec((k, tn), lambda j, i: (0, j + cb))
    else:
        w_spec = pl.BlockSpec((None, k, tn), lambda j, i: (layer, 0, j + cb))
    in_specs = a_specs + [w_spec]
    args = a_args + [w]
    if res is not None:
        in_specs.append(pl.BlockSpec((tm, tn), lambda j, i: (i, j)))
        args.append(res)
    return pl.pallas_call(
        functools.partial(_mm_body, n_a=len(a_args), ks=ks, has_res=res is not None, split=split),
        grid=(n_out // tn, m // tm),
        in_specs=in_specs,
        out_specs=pl.BlockSpec((tm, tn), lambda j, i: (i, j)),
        out_shape=jax.ShapeDtypeStruct((m, n_out), out_dtype),
        scratch_shapes=[pltpu.VMEM((k, tn), BF16)],
        compiler_params=_cparams(("arbitrary", "arbitrary")),
        name=name,
    )(*args)


def _hgrn2_chunk(q, zf, v, lbh, s, valid):
    c = q.shape[0]
    sb = min(SUB, c)
    log_lb = jnp.log(jnp.maximum(lbh, LB_FLOOR))
    lsig = _log_sigmoid(zf)
    a0 = log_lb
    a1 = jnp.log1p(-lbh) + lsig
    log_f = jnp.maximum(a0, a1) + jnp.log1p(jnp.exp(-jnp.abs(a0 - a1)))
    k = (1.0 - lbh) * _sigmoid(-zf)
    if valid is not None:
        log_f = jnp.where(valid, log_f, 0.0)
        k = jnp.where(valid, k, 0.0)
    row = lax.broadcasted_iota(jnp.int32, (c, c), 0)
    col = lax.broadcasted_iota(jnp.int32, (c, c), 1)
    tril = (col <= row).astype(F32)
    b = jnp.dot(tril, log_f, precision=lax.Precision.HIGHEST, preferred_element_type=F32)
    o_inter = _dot((q * jnp.exp(b)).astype(BF16), s.astype(BF16))
    ones = jnp.ones((LANES, LANES), BF16)
    srow = lax.broadcasted_iota(jnp.int32, (sb, LANES), 0)
    outs = []
    for i in range(c // sb):
        r0 = i * sb
        bi, qi, ki, vi = (x[r0:r0 + sb] for x in (b, q, k, v))
        ps = []
        for si in range(sb):
            diff = jnp.where(srow >= si, bi - bi[si:si + 1], NEG)
            ps.append((qi * ki[si:si + 1]) * jnp.exp(diff))
        r = _dot(jnp.concatenate(ps, axis=0).astype(BF16), ones)
        od = r[0:sb] * vi[0:1]
        for si in range(1, sb):
            od = od + r[si * sb:(si + 1) * sb] * vi[si:si + 1]
        if i > 0:
            ref = bi[0:1]
            qp = qi * jnp.exp(bi - ref)
            kp = k[:r0] * jnp.exp(ref - b[:r0])
            sc = _dot_nt(qp.astype(BF16), kp.astype(BF16))
            od = od + _dot(sc.astype(BF16), v[:r0].astype(BF16))
        outs.append(o_inter[r0:r0 + sb] + od)
    o = outs[0] if len(outs) == 1 else jnp.concatenate(outs, axis=0)
    b_last = b[c - 1:c]
    kdec = k * jnp.exp(b_last - b)
    dec_kv = jnp.transpose(jnp.broadcast_to(jnp.exp(b_last), (LANES, LANES)))
    s_new = dec_kv * s + _dot_tn(kdec.astype(BF16), v.astype(BF16))
    return o, s_new


def _hg_out(o, g, nw):
    o = o * lax.rsqrt(jnp.mean(o * o, axis=-1, keepdims=True) + EPS)
    return (o * nw) * (g * _sigmoid(g))


def _hg_prompt_body(q_ref, f_ref, v_ref, g_ref, lb_ref, nw_ref, y_ref, s_ref):
    @pl.when(pl.program_id(2) == 0)
    def _():
        s_ref[...] = jnp.zeros_like(s_ref)

    lbh = lb_ref[...]
    nw = nw_ref[...]

    def chunk(ci, carry):
        r = pl.multiple_of(ci * CHUNK, CHUNK)
        rows = pl.ds(r, CHUNK)
        o, s_new = _hgrn2_chunk(q_ref[rows, :], f_ref[rows, :], v_ref[rows, :], lbh, s_ref[0, 0], None)
        s_ref[0, 0] = s_new
        y_ref[rows, :] = _hg_out(o, g_ref[rows, :], nw).astype(y_ref.dtype)
        return carry

    lax.fori_loop(0, TB_MIX // CHUNK, chunk, 0)


def _pair_rows(x, par):
    return pltpu.roll(x, 4, 0) if par else x


def _hg_sample_body(q_ref, f_ref, v_ref, g_ref, lb_ref, nw_ref, s_in_ref, y_ref, s_ref):
    lbh = lb_ref[...]
    nw = nw_ref[...]
    rid = lax.broadcasted_iota(jnp.int32, (SUBLANES, 1), 0)
    valid = rid < DEC_SEQ
    ys = []
    for pr in range(SB_GRP // 2):
        rows = slice(SUBLANES * pr, SUBLANES * (pr + 1))
        q8, f8, v8 = q_ref[rows, :], f_ref[rows, :], v_ref[rows, :]
        os_ = []
        for par in range(2):
            j = 2 * pr + par
            o, s_new = _hgrn2_chunk(_pair_rows(q8, par), _pair_rows(f8, par), _pair_rows(v8, par),
                                    lbh, s_in_ref[j, 0], valid)
            s_ref[j, 0] = s_new
            os_.append(_pair_rows(o, par))
        o8 = jnp.where(valid, os_[0], os_[1])
        ys.append(_hg_out(o8, g_ref[rows, :], nw))
    y_ref[...] = jnp.concatenate(ys, axis=0).astype(y_ref.dtype)


def _hgrn2(z_main, lb, nw, s_sample, layer):
    lb2 = lb.reshape(1, HG_H * HG_DK)
    nw2 = nw.reshape(1, HG_W)
    nt = SEQ // TB_MIX
    col = lambda base: (lambda b, h, c: (b * nt + c, base + h))
    y_p, s_p = pl.pallas_call(
        _hg_prompt_body,
        grid=(BATCH, HG_H, nt),
        in_specs=[pl.BlockSpec((TB_MIX, LANES), col(0)),
                  pl.BlockSpec((TB_MIX, LANES), col(HG_H)),
                  pl.BlockSpec((TB_MIX, LANES), col(2 * HG_H)),
                  pl.BlockSpec((TB_MIX, LANES), col(3 * HG_H)),
                  pl.BlockSpec((1, LANES), lambda b, h, c: (0, h)),
                  pl.BlockSpec((1, LANES), lambda b, h, c: (0, h))],
        out_specs=[pl.BlockSpec((TB_MIX, LANES), lambda b, h, c: (b * nt + c, h)),
                   pl.BlockSpec((1, 1, HG_DK, HG_DV), lambda b, h, c: (b, h, 0, 0))],
        out_shape=[jax.ShapeDtypeStruct((N_PROMPT, HG_W), BF16),
                   jax.ShapeDtypeStruct((BATCH, HG_H, HG_DK, HG_DV), F32)],
        compiler_params=_cparams(("parallel", "parallel", "arbitrary")),
        name="hgrn2_prompt",
    )(z_main, z_main, z_main, z_main, lb2, nw2)

    rows = SB_GRP * DEC_SEQ
    rb0 = N_PROMPT // rows
    scol = lambda base: (lambda g, h: (rb0 + g, base + h))
    y_s, s_s = pl.pallas_call(
        _hg_sample_body,
        grid=(DEC_BATCH // SB_GRP, HG_H),
        in_specs=[pl.BlockSpec((rows, LANES), scol(0)),
                  pl.BlockSpec((rows, LANES), scol(HG_H)),
                  pl.BlockSpec((rows, LANES), scol(2 * HG_H)),
                  pl.BlockSpec((rows, LANES), scol(3 * HG_H)),
                  pl.BlockSpec((1, LANES), lambda g, h: (0, h)),
                  pl.BlockSpec((1, LANES), lambda g, h: (0, h)),
                  pl.BlockSpec((None, SB_GRP, 1, HG_DK, HG_DV), lambda g, h: (layer, g, h, 0, 0))],
        out_specs=[pl.BlockSpec((rows, LANES), lambda g, h: (g, h)),
                   pl.BlockSpec((SB_GRP, 1, HG_DK, HG_DV), lambda g, h: (g, h, 0, 0))],
        out_shape=[jax.ShapeDtypeStruct((N_SAMPLE, HG_W), BF16),
                   jax.ShapeDtypeStruct((DEC_BATCH, HG_H, HG_DK, HG_DV), F32)],
        compiler_params=_cparams(("parallel", "parallel")),
        name="hgrn2_sample",
    )(z_main, z_main, z_main, z_main, lb2, nw2, s_sample)
    return y_p, s_p, y_s, s_s


def _mlstm_chunk(q, k, v, i_pre, f_pre, cm, n, m, valid):
    c = q.shape[0]
    k = k * (ML_DK ** -0.5)
    li = GATE_CAP * jnp.tanh(i_pre / GATE_CAP)
    lf = _log_sigmoid(GATE_CAP * jnp.tanh(f_pre / GATE_CAP))
    if valid is not None:
        li = jnp.where(valid, li, NEG)
        lf = jnp.where(valid, lf, 0.0)
        k = jnp.where(valid, k, 0.0)
    row = lax.broadcasted_iota(jnp.int32, (c, c), 0)
    col = lax.broadcasted_iota(jnp.int32, (c, c), 1)
    eye = row == col
    causal = col <= row
    lf_row = jnp.sum(jnp.where(eye, lf, 0.0), axis=0, keepdims=True)
    li_row = jnp.sum(jnp.where(eye, li, 0.0), axis=0, keepdims=True)
    b_col = jnp.sum(jnp.where(causal, lf_row, 0.0), axis=1, keepdims=True)
    b_row = jnp.sum(jnp.where(row <= col, lf, 0.0), axis=0, keepdims=True)
    d = jnp.where(causal, b_col - b_row + li_row, NEG)
    inter = b_col + m
    m_t = jnp.maximum(inter, jnp.max(d, axis=1, keepdims=True))
    w_inter = jnp.exp(inter - m_t)
    qb = q.astype(BF16)
    w_intra = jnp.exp(d - m_t) * _dot_nt(qb, k.astype(BF16))
    num = w_inter * _dot(qb, cm.astype(BF16)) + _dot(w_intra.astype(BF16), v.astype(BF16))
    den = w_inter * jnp.sum(q * n, axis=1, keepdims=True) + jnp.sum(w_intra, axis=1, keepdims=True)
    h = num / jnp.maximum(jnp.abs(den), jnp.exp(-m_t))
    b_last = b_col[c - 1:c]
    m_last = m_t[c - 1:c]
    w_prev = jnp.exp(b_last + m - m_last)
    w_new = jnp.exp(b_last - b_col + li - m_last)
    kw = w_new * k
    cm_new = w_prev * cm + _dot_tn(kw.astype(BF16), v.astype(BF16))
    n_new = w_prev * n + jnp.sum(kw, axis=0, keepdims=True)
    return h, cm_new, n_new, m_last


def _ml_gates(gate, bias, h):
    lane = lax.broadcasted_iota(jnp.int32, gate.shape, 1)
    g = gate + bias
    i_pre = jnp.sum(jnp.where(lane == h, g, 0.0), axis=1, keepdims=True)
    f_pre = jnp.sum(jnp.where(lane == h + ML_H, g, 0.0), axis=1, keepdims=True)
    return i_pre, f_pre


def _ml_out(h, og, nw):
    h = h * lax.rsqrt(jnp.mean(h * h, axis=-1, keepdims=True) + EPS)
    return (h * nw) * _sigmoid(og)


def _ml_prompt_body(q_ref, k_ref, v_ref, o_ref, gate_ref, bias_ref, nw_ref, y_ref, c_ref, n_ref, m_ref):
    @pl.when(pl.program_id(2) == 0)
    def _():
        c_ref[...] = jnp.zeros_like(c_ref)
        n_ref[...] = jnp.zeros_like(n_ref)
        m_ref[...] = jnp.zeros_like(m_ref)

    hh = pl.program_id(1)
    bias = bias_ref[...]
    nw = nw_ref[...]

    def chunk(ci, carry):
        r = pl.multiple_of(ci * CHUNK, CHUNK)
        rows = pl.ds(r, CHUNK)
        i_pre, f_pre = _ml_gates(gate_ref[rows, :], bias, hh)
        h, cm, n, m = _mlstm_chunk(q_ref[rows, :], k_ref[rows, :], v_ref[rows, :], i_pre, f_pre,
                                   c_ref[0, 0], n_ref[0, 0], m_ref[0, 0][:, 0:1], None)
        c_ref[0, 0] = cm
        n_ref[0, 0] = n
        m_ref[0, 0] = jnp.broadcast_to(m, (1, LANES))
        y_ref[rows, :] = _ml_out(h, o_ref[rows, :], nw).astype(y_ref.dtype)
        return carry

    lax.fori_loop(0, TB_MIX // CHUNK, chunk, 0)


def _ml_sample_body(q_ref, k_ref, v_ref, o_ref, gate_ref, bias_ref, nw_ref, c_in_ref, n_in_ref, m_in_ref,
                    y_ref, c_ref, n_ref, m_ref):
    hh = pl.program_id(1)
    bias = bias_ref[...]
    nw = nw_ref[...]
    rid = lax.broadcasted_iota(jnp.int32, (SUBLANES, 1), 0)
    valid = rid < DEC_SEQ
    ys = []
    for pr in range(SB_GRP // 2):
        rows = slice(SUBLANES * pr, SUBLANES * (pr + 1))
        q8, k8, v8 = q_ref[rows, :], k_ref[rows, :], v_ref[rows, :]
        i8, f8 = _ml_gates(gate_ref[rows, :], bias, hh)
        hs = []
        for par in range(2):
            j = 2 * pr + par
            h, cm, n, m = _mlstm_chunk(_pair_rows(q8, par), _pair_rows(k8, par), _pair_rows(v8, par),
                                       _pair_rows(i8, par), _pair_rows(f8, par),
                                       c_in_ref[j, 0], n_in_ref[j, 0], m_in_ref[j, 0][:, 0:1], valid)
            c_ref[j, 0] = cm
            n_ref[j, 0] = n
            m_ref[j, 0] = jnp.broadcast_to(m, (1, LANES))
            hs.append(_pair_rows(h, par))
        h8 = jnp.where(valid, hs[0], hs[1])
        ys.append(_ml_out(h8, o_ref[rows, :], nw))
    y_ref[...] = jnp.concatenate(ys, axis=0).astype(y_ref.dtype)


def _mlstm(z_main, z_gate, if_b, nw, c_s, n_s, m_s, layer):
    bias = jnp.zeros((1, LANES), F32).at[0, :N_GATE].set(if_b)
    nw2 = nw.reshape(1, ML_W)
    nt = SEQ // TB_MIX
    qb = (2 * HG_H * HG_DK + 2 * HG_W) // LANES
    kb = qb + ML_H
    vb = (qb + 2 * ML_H) * LANES // ML_DV
    ob = vb + ML_H
    pcol = lambda base: (lambda b, h, c: (b * nt + c, base + h))
    st = lambda shape: pl.BlockSpec((1, 1) + shape, lambda b, h, c: (b, h, 0, 0))
    y_p, c_p, n_p, m_p = pl.pallas_call(
        _ml_prompt_body,
        grid=(BATCH, ML_H, nt),
        in_specs=[pl.BlockSpec((TB_MIX, ML_DK), pcol(qb)),
                  pl.BlockSpec((TB_MIX, ML_DK), pcol(kb)),
                  pl.BlockSpec((TB_MIX, ML_DV), pcol(vb)),
                  pl.BlockSpec((TB_MIX, ML_DV), pcol(ob)),
                  pl.BlockSpec((TB_MIX, LANES), lambda b, h, c: (b * nt + c, 0)),
                  pl.BlockSpec((1, LANES), lambda b, h, c: (0, 0)),
                  pl.BlockSpec((1, ML_DV), lambda b, h, c: (0, h))],
        out_specs=[pl.BlockSpec((TB_MIX, ML_DV), lambda b, h, c: (b * nt + c, h)),
                   st((ML_DK, ML_DV)), st((1, ML_DK)), st((1, LANES))],
        out_shape=[jax.ShapeDtypeStruct((N_PROMPT, ML_W), BF16),
                   jax.ShapeDtypeStruct((BATCH, ML_H, ML_DK, ML_DV), F32),
                   jax.ShapeDtypeStruct((BATCH, ML_H, 1, ML_DK), F32),
                   jax.ShapeDtypeStruct((BATCH, ML_H, 1, LANES), F32)],
        compiler_params=_cparams(("parallel", "parallel", "arbitrary")),
        name="mlstm_prompt",
    )(z_main, z_main, z_main, z_main, z_gate, bias, nw2)

    rows = SB_GRP * DEC_SEQ
    rb0 = N_PROMPT // rows
    scol = lambda base: (lambda g, h: (rb0 + g, base + h))
    sst = lambda shape: pl.BlockSpec((SB_GRP, 1) + shape, lambda g, h: (g, h, 0, 0))
    n_in = n_s.reshape(DEC_BATCH, ML_H, 1, ML_DK)
    m_in = jnp.broadcast_to(m_s[:, :, None, None], (DEC_BATCH, ML_H, 1, LANES))
    y_s, c_o, n_o, m_o = pl.pallas_call(
        _ml_sample_body,
        grid=(DEC_BATCH // SB_GRP, ML_H),
        in_specs=[pl.BlockSpec((rows, ML_DK), scol(qb)),
                  pl.BlockSpec((rows, ML_DK), scol(kb)),
                  pl.BlockSpec((rows, ML_DV), scol(vb)),
                  pl.BlockSpec((rows, ML_DV), scol(ob)),
                  pl.BlockSpec((rows, LANES), lambda g, h: (rb0 + g, 0)),
                  pl.BlockSpec((1, LANES), lambda g, h: (0, 0)),
                  pl.BlockSpec((1, ML_DV), lambda g, h: (0, h)),
                  pl.BlockSpec((None, SB_GRP, 1, ML_DK, ML_DV), lambda g, h: (layer, g, h, 0, 0)),
                  sst((1, ML_DK)), sst((1, LANES))],
        out_specs=[pl.BlockSpec((rows, ML_DV), lambda g, h: (g, h)),
                   sst((ML_DK, ML_DV)), sst((1, ML_DK)), sst((1, LANES))],
        out_shape=[jax.ShapeDtypeStruct((N_SAMPLE, ML_W), BF16),
                   jax.ShapeDtypeStruct((DEC_BATCH, ML_H, ML_DK, ML_DV), F32),
                   jax.ShapeDtypeStruct((DEC_BATCH, ML_H, 1, ML_DK), F32),
                   jax.ShapeDtypeStruct((DEC_BATCH, ML_H, 1, LANES), F32)],
        compiler_params=_cparams(("parallel", "parallel")),
        name="mlstm_sample",
    )(z_main, z_main, z_main, z_main, z_gate, bias, nw2, c_s, n_in, m_in)
    return (y_p, c_p, n_p[:, :, 0, :], m_p[:, :, 0, 0]), (y_s, c_o, n_o[:, :, 0, :], m_o[:, :, 0, 0])


def _gelu_tanh(x):
    return 0.5 * x * (1.0 + jnp.tanh(np.sqrt(2.0 / np.pi).astype(np.float32) * (x + 0.044715 * (x * x * x))))


def _rg_gates(u, wbd_ref, bias, lam):
    ri = _dot(u.astype(BF16), wbd_ref[...]) + bias
    r = _sigmoid(ri[:, :RG_W])
    i = _sigmoid(ri[:, RG_W:])
    log_a = -RG_C * r * _softplus(-lam)
    a = jnp.exp(log_a)
    b_in = jnp.sqrt(-_expm1(2.0 * log_a)) * (i * u)
    return a, b_in


def _rg_prompt_body(x_ref, g_ref, cw_ref, cb_ref, wbd_ref, bias_ref, lam_ref, y_ref, h_ref, buf_ref,
                    xp_ref, a_ref, b_ref):
    tb = x_ref.shape[0]

    @pl.when(pl.program_id(1) == 0)
    def _():
        xp_ref[0:SUBLANES, :] = jnp.zeros((SUBLANES, RG_W), F32)
        h_ref[...] = jnp.zeros_like(h_ref)

    x = x_ref[...]
    xp_ref[SUBLANES:, :] = x
    u = cb_ref[...] + x * cw_ref[CONV_W - 1:CONV_W, :]
    for j in range(CONV_W - 1):
        sh = CONV_W - 1 - j
        u = u + xp_ref[SUBLANES - sh:SUBLANES - sh + tb, :] * cw_ref[j:j + 1, :]
    a, b_in = _rg_gates(u, wbd_ref, bias_ref[...], lam_ref[...])
    a_ref[...] = a
    b_ref[...] = b_in
    rid = lax.broadcasted_iota(jnp.int32, (SUBLANES, RG_W), 0)

    def step(i, h):
        r = pl.multiple_of(i * SUBLANES, SUBLANES)
        a8 = a_ref[pl.ds(r, SUBLANES), :]
        b8 = b_ref[pl.ds(r, SUBLANES), :]
        h8 = jnp.zeros((SUBLANES, RG_W), F32)
        for rr in range(SUBLANES):
            h = a8[rr:rr + 1] * h + b8[rr:rr + 1]
            h8 = jnp.where(rid == rr, h, h8)
        b_ref[pl.ds(r, SUBLANES), :] = h8
        return h

    h_last = lax.fori_loop(0, tb // SUBLANES, step, h_ref[0])
    h_ref[0] = h_last
    y_ref[...] = (b_ref[...] * _gelu_tanh(g_ref[...])).astype(y_ref.dtype)
    buf_ref[0] = x[tb - (CONV_W - 1):, :]
    xp_ref[0:SUBLANES, :] = x[tb - SUBLANES:, :]


def _rg_sample_body(x_ref, g_ref, buf_in_ref, h_in_ref, cw_ref, cb_ref, wbd_ref, bias_ref, lam_ref,
                    y_ref, h_ref, buf_ref):
    xs = [buf_in_ref[j] for j in range(CONV_W - 1)] + [x_ref[t] for t in range(DEC_SEQ)]
    h = h_in_ref[...]
    for t in range(DEC_SEQ):
        u = cb_ref[...]
        for j in range(CONV_W):
            u = u + xs[t + j] * cw_ref[j:j + 1, :]
        a, b_in = _rg_gates(u, wbd_ref, bias_ref[...], lam_ref[...])
        h = a * h + b_in
        y_ref[t] = h * _gelu_tanh(g_ref[t])
    h_ref[...] = h
    for j in range(CONV_W - 1):
        buf_ref[j] = xs[DEC_SEQ + j]


def _block_diag(w):
    eye = jnp.eye(RG_BLOCKS, dtype=w.dtype)
    return (eye[:, None, :, None] * w[:, :, None, :]).reshape(RG_W, RG_W)


def _rglru(z_rg, lp, h_s, buf_s):
    wbd = jnp.concatenate([_block_diag(lp['rg_wa']), _block_diag(lp['rg_wx'])], axis=1).astype(BF16)
    bias = jnp.concatenate([lp['rg_ba'], lp['rg_bx']]).reshape(1, 2 * RG_W)
    cw = lp['rg_conv_w']
    cb = lp['rg_conv_b'].reshape(1, RG_W)
    lam = lp['rg_lambda'].reshape(1, RG_W)
    tb = TB_MIX
    nt = SEQ // tb
    full = lambda shape: pl.BlockSpec(shape, lambda *_: (0,) * len(shape))
    y_p, h_p, buf_p = pl.pallas_call(
        _rg_prompt_body,
        grid=(BATCH, nt),
        in_specs=[pl.BlockSpec((tb, RG_W), lambda b, c: (b * nt + c, 0)),
                  pl.BlockSpec((tb, RG_W), lambda b, c: (b * nt + c, 1)),
                  full((CONV_W, RG_W)), full((1, RG_W)), full((RG_W, 2 * RG_W)), full((1, 2 * RG_W)),
                  full((1, RG_W))],
        out_specs=[pl.BlockSpec((tb, RG_W), lambda b, c: (b * nt + c, 0)),
                   pl.BlockSpec((1, 1, RG_W), lambda b, c: (b, 0, 0)),
                   pl.BlockSpec((1, CONV_W - 1, RG_W), lambda b, c: (b, 0, 0))],
        out_shape=[jax.ShapeDtypeStruct((N_PROMPT, RG_W), BF16),
                   jax.ShapeDtypeStruct((BATCH, 1, RG_W), F32),
                   jax.ShapeDtypeStruct((BATCH, CONV_W - 1, RG_W), F32)],
        scratch_shapes=[pltpu.VMEM((tb + SUBLANES, RG_W), F32),
                        pltpu.VMEM((tb, RG_W), F32),
                        pltpu.VMEM((tb, RG_W), F32)],
        compiler_params=_cparams(("parallel", "arbitrary")),
        name="rglru_prompt",
    )(z_rg, z_rg, cw, cb, wbd, bias, lam)

    zs = z_rg[N_PROMPT:].reshape(DEC_BATCH, DEC_SEQ, 2 * RG_W).transpose(1, 0, 2)
    y_s, h_o, buf_o = pl.pallas_call(
        _rg_sample_body,
        grid=(1,),
        in_specs=[pl.BlockSpec((DEC_SEQ, DEC_BATCH, RG_W), lambda i: (0, 0, 0)),
                  pl.BlockSpec((DEC_SEQ, DEC_BATCH, RG_W), lambda i: (0, 0, 1)),
                  full((CONV_W - 1, DEC_BATCH, RG_W)), full((DEC_BATCH, RG_W)),
                  full((CONV_W, RG_W)), full((1, RG_W)), full((RG_W, 2 * RG_W)), full((1, 2 * RG_W)),
                  full((1, RG_W))],
        out_specs=[full((DEC_SEQ, DEC_BATCH, RG_W)), full((DEC_BATCH, RG_W)),
                   full((CONV_W - 1, DEC_BATCH, RG_W))],
        out_shape=[jax.ShapeDtypeStruct((DEC_SEQ, DEC_BATCH, RG_W), F32),
                   jax.ShapeDtypeStruct((DEC_BATCH, RG_W), F32),
                   jax.ShapeDtypeStruct((CONV_W - 1, DEC_BATCH, RG_W), F32)],
        compiler_params=_cparams(("arbitrary",)),
        name="rglru_sample",
    )(zs, zs, buf_s.transpose(1, 0, 2), h_s, cw, cb, wbd, bias, lam)
    y_s = y_s.transpose(1, 0, 2).reshape(N_SAMPLE, RG_W).astype(BF16)
    return (y_p, h_p[:, 0, :], buf_p), (y_s, h_o, buf_o.transpose(1, 0, 2))


def _attn_body(q_ref, k_ref, v_ref, o_ref, *, nb, tq):
    q = q_ref[...]
    k = k_ref[...].astype(BF16)
    v = v_ref[...].astype(BF16)
    s = _dot_nt(q, k) * (XA_DH ** -0.5)
    if nb > 1:
        rb = lax.broadcasted_iota(jnp.int32, s.shape, 0) // tq
        cb = lax.broadcasted_iota(jnp.int32, s.shape, 1) // N_MEM
        s = jnp.where(rb == cb, s, NEG)
    p = jnp.exp(s - jnp.max(s, axis=-1, keepdims=True))
    p = p / jnp.sum(p, axis=-1, keepdims=True)
    o_ref[...] = _dot(p.astype(BF16), v).astype(o_ref.dtype)


def _cross_attend(q, mk_p, mv_p, mk_s, mv_s, layer):
    nt = SEQ // TQ
    o_p = pl.pallas_call(
        functools.partial(_attn_body, nb=1, tq=TQ),
        grid=(BATCH, XA_H, nt),
        in_specs=[pl.BlockSpec((TQ, XA_DH), lambda b, h, c: (b * nt + c, h)),
                  pl.BlockSpec((N_MEM, XA_DH), lambda b, h, c: (b, h)),
                  pl.BlockSpec((N_MEM, XA_DH), lambda b, h, c: (b, h))],
        out_specs=pl.BlockSpec((TQ, XA_DH), lambda b, h, c: (b * nt + c, h)),
        out_shape=jax.ShapeDtypeStruct((N_PROMPT, XA_W), BF16),
        compiler_params=_cparams(("parallel", "parallel", "parallel")),
        name="xattn_prompt",
    )(q, mk_p, mv_p)
    nb = 4
    rows = nb * DEC_SEQ
    rb0 = N_PROMPT // rows
    o_s = pl.pallas_call(
        functools.partial(_attn_body, nb=nb, tq=DEC_SEQ),
        grid=(DEC_BATCH // nb, XA_H),
        in_specs=[pl.BlockSpec((rows, XA_DH), lambda g, h: (rb0 + g, h)),
                  pl.BlockSpec((nb * N_MEM, XA_DH), lambda g, h: (layer * (DEC_BATCH // nb) + g, h)),
                  pl.BlockSpec((nb * N_MEM, XA_DH), lambda g, h: (layer * (DEC_BATCH // nb) + g, h))],
        out_specs=pl.BlockSpec((rows, XA_DH), lambda g, h: (g, h)),
        out_shape=jax.ShapeDtypeStruct((N_SAMPLE, XA_W), BF16),
        compiler_params=_cparams(("parallel", "parallel")),
        name="xattn_sample",
    )(q, mk_s, mv_s)
    return o_p, o_s


MOE_ROWS = ((MOE_K * N_TOK + MOE_NE * (TM_MOE - 1)) // TM_MOE) * TM_MOE
D_PACK = D_MODEL // 2
RT_E0, RT_E1, RT_W0, RT_W1, RT_R0, RT_R1 = range(6)


def _lane_pick(vals, lane):
    out = jnp.zeros(lane.shape, F32)
    for i, v in enumerate(vals):
        out = jnp.where(lane == i, v, out)
    return out


def _router_body(x_ref, w_ref, wg_ref, bg_ref, xp_ref, rt_ref, cnt_ref, tril_ref):
    tm = x_ref.shape[0]

    @pl.when(pl.program_id(0) == 0)
    def _():
        cnt_ref[...] = jnp.zeros_like(cnt_ref)
        r = lax.broadcasted_iota(jnp.int32, (tm, tm), 0)
        c = lax.broadcasted_iota(jnp.int32, (tm, tm), 1)
        tril_ref[...] = (c < r).astype(BF16)

    x = x_ref[...]
    ms = jnp.mean(x * x, axis=-1, keepdims=True)
    xn = (x * lax.rsqrt(ms + EPS)) * w_ref[...]
    lo = pltpu.bitcast(xn[:, :D_PACK].astype(BF16).astype(F32), jnp.uint32)
    hi = pltpu.bitcast(xn[:, D_PACK:].astype(BF16).astype(F32), jnp.uint32)
    xp_ref[...] = (hi & jnp.uint32(0xFFFF0000)) | (lo >> 16)

    lg = jnp.dot(xn, wg_ref[...], precision=lax.Precision.HIGHEST, preferred_element_type=F32) + bg_ref[...]
    lane_i = lax.broadcasted_iota(jnp.int32, (tm, LANES), 1)
    lane = lane_i.astype(F32)
    big = float(LANES)
    is_g = lane_i < MOE_G
    gm = jnp.max(jnp.where(is_g, lg, NEG), axis=1, keepdims=True)
    g_w = 1.0 / jnp.sum(jnp.where(is_g, jnp.exp(lg - gm), 0.0), axis=1, keepdims=True)
    g_idx = jnp.min(jnp.where(is_g & (lg == gm), lane, big), axis=1, keepdims=True)
    e_lane = lane - float(MOE_G)
    sel = (e_lane >= g_idx * MOE_E) & (e_lane < (g_idx + 1.0) * MOE_E)
    v0 = jnp.max(jnp.where(sel, lg, NEG), axis=1, keepdims=True)
    i0 = jnp.min(jnp.where(sel & (lg == v0), lane, big), axis=1, keepdims=True)
    sel1 = sel & (lane != i0)
    v1 = jnp.max(jnp.where(sel1, lg, NEG), axis=1, keepdims=True)
    i1 = jnp.min(jnp.where(sel1 & (lg == v1), lane, big), axis=1, keepdims=True)
    e0 = i0 - float(MOE_G)
    e1 = i1 - float(MOE_G)
    t = jnp.exp(v1 - v0)
    w0 = g_w / (1.0 + t)
    w1 = g_w * t / (1.0 + t)
    hit0 = lane == e0
    hit1 = lane == e1
    oh = jnp.where(hit0 | hit1, 1.0, 0.0)
    before = cnt_ref[...] + _dot(tril_ref[...], oh.astype(BF16))
    r0 = jnp.sum(jnp.where(hit0, before, 0.0), axis=1, keepdims=True)
    r1 = jnp.sum(jnp.where(hit1, before, 0.0), axis=1, keepdims=True)
    cnt_ref[...] += jnp.sum(oh, axis=0, keepdims=True)
    rt_ref[...] = _lane_pick((e0, e1, w0, w1, r0, r1), lane_i)


def _router(x, norm_w, wg, bg, tm):
    n, d = x.shape
    return pl.pallas_call(
        _router_body,
        grid=(n // tm,),
        in_specs=[pl.BlockSpec((tm, d), lambda i: (i, 0)),
                  pl.BlockSpec((1, d), lambda i: (0, 0)),
                  pl.BlockSpec((d, LANES), lambda i: (0, 0)),
                  pl.BlockSpec((1, LANES), lambda i: (0, 0))],
        out_specs=[pl.BlockSpec((tm, D_PACK), lambda i: (i, 0)),
                   pl.BlockSpec((tm, LANES), lambda i: (i, 0)),
                   pl.BlockSpec((1, LANES), lambda i: (0, 0))],
        out_shape=[jax.ShapeDtypeStruct((n, D_PACK), jnp.uint32),
                   jax.ShapeDtypeStruct((n, LANES), F32),
                   jax.ShapeDtypeStruct((1, LANES), F32)],
        scratch_shapes=[pltpu.VMEM((tm, tm), BF16)],
        compiler_params=_cparams(("arbitrary",)),
        name="moe_router",
    )(x, norm_w.reshape(1, d), wg, bg)


def _row_copy(src_ref, src_row, dst_ref, dst_row, sem):
    return pltpu.make_async_copy(src_ref.at[pl.ds(src_row, 1), :], dst_ref.at[pl.ds(dst_row, 1), :], sem)


def _dispatch_body(pos_ref, xp_ref, xs_in_ref, xs_ref, sem):
    del xs_in_ref
    tm = xp_ref.shape[0]

    def start(r, c):
        for k in range(MOE_K):
            _row_copy(xp_ref, r, xs_ref, pos_ref[0, 0, MOE_K * r + k], sem).start()
        return c

    lax.fori_loop(0, tm, start, 0)

    def wait(r, c):
        for k in range(MOE_K):
            _row_copy(xp_ref, r, xs_ref, pos_ref[0, 0, MOE_K * r + k], sem).wait()
        return c

    lax.fori_loop(0, tm, wait, 0)


def _dispatch(pos3, xp, tm):
    n = xp.shape[0]
    xs0 = jnp.zeros((MOE_ROWS, D_PACK), jnp.uint32)
    return pl.pallas_call(
        _dispatch_body,
        grid=(n // tm,),
        in_specs=[pl.BlockSpec((1, 1, MOE_K * tm), lambda i: (i, 0, 0), memory_space=pltpu.SMEM),
                  pl.BlockSpec((tm, D_PACK), lambda i: (i, 0)),
                  pl.BlockSpec(memory_space=pl.ANY)],
        out_specs=pl.BlockSpec(memory_space=pl.ANY),
        out_shape=jax.ShapeDtypeStruct((MOE_ROWS, D_PACK), jnp.uint32),
        scratch_shapes=[pltpu.SemaphoreType.DMA(())],
        input_output_aliases={2: 0},
        compiler_params=_cparams(("arbitrary",)),
        name="moe_dispatch",
    )(pos3, xp, xs0)


def _expert_body(te_ref, nu_ref, x_ref, w1_ref, w3_ref, w2_ref, o_ref, w1b, w3b, w2b):
    t = pl.program_id(0)
    prev = te_ref[jnp.maximum(t - 1, 0)]

    @pl.when((t == 0) | (te_ref[t] != prev))
    def _():
        w1b[...] = w1_ref[...].astype(BF16)
        w3b[...] = w3_ref[...].astype(BF16)
        w2b[...] = w2_ref[...].astype(BF16)

    @pl.when(t < nu_ref[0])
    def _():
        xp = x_ref[...]
        lo = pltpu.bitcast(xp << 16, F32).astype(BF16)
        hi = pltpu.bitcast(xp & jnp.uint32(0xFFFF0000), F32).astype(BF16)
        hg = _dot(lo, w1b[:D_PACK, :]) + _dot(hi, w1b[D_PACK:, :])
        hu = _dot(lo, w3b[:D_PACK, :]) + _dot(hi, w3b[D_PACK:, :])
        act = (hg * _sigmoid(hg)) * hu
        o_ref[...] = _dot(act.astype(BF16), w2b[...])

    @pl.when(t >= nu_ref[0])
    def _():
        o_ref[...] = jnp.zeros_like(o_ref)


def _experts(tile_expert, n_used, xs, w1, w3, w2, layer):
    r = xs.shape[0]
    d = D_MODEL
    nt = r // TM_MOE
    wspec = lambda shape: pl.BlockSpec((None, None) + shape, lambda t, te, nu: (layer, te[t], 0, 0))
    grid_spec = pltpu.PrefetchScalarGridSpec(
        num_scalar_prefetch=2,
        grid=(nt,),
        in_specs=[pl.BlockSpec((TM_MOE, D_PACK), lambda t, te, nu: (t, 0)),
                  wspec((d, MOE_F)), wspec((d, MOE_F)), wspec((MOE_F, d))],
        out_specs=pl.BlockSpec((TM_MOE, d), lambda t, te, nu: (t, 0)),
        scratch_shapes=[pltpu.VMEM((d, MOE_F), BF16), pltpu.VMEM((d, MOE_F), BF16),
                        pltpu.VMEM((MOE_F, d), BF16)],
    )
    return pl.pallas_call(
        _expert_body,
        grid_spec=grid_spec,
        out_shape=jax.ShapeDtypeStruct((r, d), F32),
        compiler_params=_cparams(("arbitrary",)),
        name="moe_experts",
    )(tile_expert, n_used, xs, w1, w3, w2)


def _combine_body(pos_ref, x_ref, rt_ref, ys_ref, o_ref, g0_ref, g1_ref, sem):
    tm = x_ref.shape[0]
    bufs = (g0_ref, g1_ref)

    def start(r, c):
        for k in range(MOE_K):
            _row_copy(ys_ref, pos_ref[0, 0, MOE_K * r + k], bufs[k], r, sem).start()
        return c

    lax.fori_loop(0, tm, start, 0)

    def wait(r, c):
        for k in range(MOE_K):
            _row_copy(ys_ref, pos_ref[0, 0, MOE_K * r + k], bufs[k], r, sem).wait()
        return c

    lax.fori_loop(0, tm, wait, 0)
    rt = rt_ref[...]
    w0 = rt[:, RT_W0:RT_W0 + 1]
    w1 = rt[:, RT_W1:RT_W1 + 1]
    o_ref[...] = x_ref[...] + w0 * g0_ref[...] + w1 * g1_ref[...]


def _combine(pos3, x, rt, ys, tm):
    n, d = x.shape
    return pl.pallas_call(
        _combine_body,
        grid=(n // tm,),
        in_specs=[pl.BlockSpec((1, 1, MOE_K * tm), lambda i: (i, 0, 0), memory_space=pltpu.SMEM),
                  pl.BlockSpec((tm, d), lambda i: (i, 0)),
                  pl.BlockSpec((tm, LANES), lambda i: (i, 0)),
                  pl.BlockSpec(memory_space=pl.ANY)],
        out_specs=pl.BlockSpec((tm, d), lambda i: (i, 0)),
        out_shape=jax.ShapeDtypeStruct((n, d), F32),
        scratch_shapes=[pltpu.VMEM((tm, d), F32), pltpu.VMEM((tm, d), F32), pltpu.SemaphoreType.DMA(())],
        compiler_params=_cparams(("arbitrary",)),
        name="moe_combine",
    )(pos3, x, rt, ys)


def _dispatch_plan(rt, cnt):
    counts = cnt[0, :MOE_NE].astype(jnp.int32)
    padded = ((counts + TM_MOE - 1) // TM_MOE) * TM_MOE
    ids = jnp.arange(MOE_NE, dtype=jnp.int32)
    ends = jnp.sum(jnp.where(ids[None, :] <= ids[:, None], padded[None, :], 0), axis=1)
    starts = ends - padded
    e = rt[:, RT_E0:RT_E1 + 1].astype(jnp.int32)
    rank = rt[:, RT_R0:RT_R1 + 1].astype(jnp.int32)
    pos = rank + jnp.sum(jnp.where(e[:, :, None] == ids[None, None, :], starts[None, None, :], 0), axis=-1)
    n_used = ends[MOE_NE - 1] // TM_MOE
    tile = jnp.minimum(jnp.arange(MOE_ROWS // TM_MOE, dtype=jnp.int32), n_used - 1)
    tile_expert = jnp.sum((tile[:, None] * TM_MOE >= ends[None, :]).astype(jnp.int32), axis=1)
    return pos, tile_expert, n_used.reshape(1)


def _hier_moe(x, lp, w1, w3, w2, layer):
    wg = jnp.zeros((D_MODEL, LANES), F32)
    wg = wg.at[:, :MOE_G].set(lp['moe_wg_grp']).at[:, MOE_G:MOE_G + MOE_NE].set(lp['moe_wg_exp'])
    bg = jnp.zeros((1, LANES), F32)
    bg = bg.at[0, :MOE_G].set(lp['moe_bg_grp']).at[0, MOE_G:MOE_G + MOE_NE].set(lp['moe_bg_exp'])
    tm = TM_SPLIT
    xp, rt, cnt = _router(x, lp['norm_ffn_w'], wg, bg, tm)
    pos, tile_expert, n_used = _dispatch_plan(rt, cnt)
    pos3 = pos.reshape(N_TOK // tm, 1, MOE_K * tm)
    xs = _dispatch(pos3, xp, tm)
    ys = _experts(tile_expert, n_used, xs, w1, w3, w2, layer)
    return _combine(pos3, x, rt, ys, tm)


def _decoder_layer(x, layer, lp, big, lb, mem_p):
    xn = _rmsnorm(x, lp['norm_mix_w'], BF16, TM // 2)
    z_main = _matmul(xn, big['w_in'], layer, n_out=Z_MAIN, tm=TM, tn=TN, name="w_in_main")
    z_gate = _matmul(xn, big['w_gate'], layer, tm=TM, tn=LANES, name="w_in_gate")
    z_rg = _matmul(xn, big['w_rg'], layer, tm=TM, tn=TN, name="w_in_rg")

    ya_p, hg_p, ya_s, hg_s = _hgrn2(z_main, lb, lp['hg_norm_w'], big['state_hg'], layer)
    (yb_p, mlc_p, mln_p, mlm_p), (yb_s, mlc_s, mln_s, mlm_s) = _mlstm(
        z_main, z_gate, lp['ml_if_b'], lp['ml_norm_w'], big['state_ml_c'], lp['state_ml_n'],
        lp['state_ml_m'], layer)
    (yc_p, rgh_p, rgc_p), (yc_s, rgh_s, rgc_s) = _rglru(z_rg, lp, lp['state_rg_h'], lp['state_rg_conv'])
    x = _matmul([(ya_p, ya_s), (yb_p, yb_s), (yc_p, yc_s)], big['w_out'], layer, tn=TN, tm=None, res=x,
                name="w_out")

    mn = _rmsnorm(mem_p, lp['norm_mem_w'], BF16, BATCH * N_MEM // 2)
    mk_p = _matmul(mn, big['xa_wk'], layer, tm=BATCH * N_MEM, tn=XA_W, name="mem_k")
    mv_p = _matmul(mn, big['xa_wv'], layer, tm=BATCH * N_MEM, tn=XA_W, name="mem_v")

    xn = _rmsnorm(x, lp['norm_xa_w'], BF16, TM // 2)
    q = _matmul(xn, big['xa_wq'], layer, tm=TM, tn=XA_W, out_dtype=BF16, name="xa_q")
    o_p, o_s = _cross_attend(q, mk_p, mv_p, big['cache_k'], big['cache_v'], layer)
    x = _matmul([(o_p, o_s)], big['xa_wo'], layer, tn=TN, tm=None, res=x, name="xa_o")

    x = _hier_moe(x, lp, big['moe_w1'], big['moe_w3'], big['moe_w2'], layer)
    st_p = (hg_p, mlc_p, mln_p, mlm_p, rgh_p, rgc_p)
    st_s = (hg_s, mlc_s, mln_s, mlm_s, rgh_s, rgc_s)
    return x, mk_p, mv_p, st_p, st_s


def kernel(x_prompt, x_sample, mem_prompt, cache_mem_k, cache_mem_v, state_hg, state_ml_c, state_ml_n, state_ml_m, state_rg_h, state_rg_conv, norm_mix_w, w_in, hg_lb, hg_norm_w, ml_if_b, ml_norm_w, rg_conv_w, rg_conv_b, rg_wa, rg_ba, rg_wx, rg_bx, rg_lambda, w_out, norm_xa_w, norm_mem_w, xa_wq, xa_wk, xa_wv, xa_wo, norm_ffn_w, moe_wg_grp, moe_bg_grp, moe_wg_exp, moe_bg_exp, moe_w1, moe_w3, moe_w2, norm_final_w):
    p_lb = jax.nn.softmax(hg_lb.astype(F32), axis=0)
    lb_all = jnp.cumsum(p_lb, axis=0) - p_lb[0]
    x = jnp.concatenate([x_prompt.reshape(N_PROMPT, D_MODEL), x_sample.reshape(N_SAMPLE, D_MODEL)], axis=0)
    mem_p = mem_prompt.reshape(BATCH * N_MEM, D_MODEL)
    small = dict(norm_mix_w=norm_mix_w, hg_norm_w=hg_norm_w, ml_if_b=ml_if_b, ml_norm_w=ml_norm_w,
                 rg_conv_w=rg_conv_w, rg_conv_b=rg_conv_b, rg_wa=rg_wa, rg_ba=rg_ba, rg_wx=rg_wx,
                 rg_bx=rg_bx, rg_lambda=rg_lambda, norm_xa_w=norm_xa_w, norm_mem_w=norm_mem_w,
                 norm_ffn_w=norm_ffn_w, moe_wg_grp=moe_wg_grp, moe_bg_grp=moe_bg_grp,
                 moe_wg_exp=moe_wg_exp, moe_bg_exp=moe_bg_exp, state_ml_n=state_ml_n,
                 state_ml_m=state_ml_m, state_rg_h=state_rg_h, state_rg_conv=state_rg_conv)
    w_gate = jnp.zeros((DEPTH, D_MODEL, LANES), F32).at[:, :, :N_GATE].set(w_in[:, :, Z_MAIN:Z_MAIN + N_GATE])
    big = dict(w_in=w_in, w_gate=w_gate, w_rg=w_in[:, :, Z_MAIN + N_GATE:], w_out=w_out, xa_wq=xa_wq,
               xa_wk=xa_wk, xa_wv=xa_wv, xa_wo=xa_wo, moe_w1=moe_w1, moe_w3=moe_w3, moe_w2=moe_w2,
               state_hg=state_hg, state_ml_c=state_ml_c,
               cache_k=cache_mem_k.reshape(DEPTH * DEC_BATCH * N_MEM, XA_W),
               cache_v=cache_mem_v.reshape(DEPTH * DEC_BATCH * N_MEM, XA_W))
    mem_k, mem_v, st_p, st_s = [], [], [], []
    for l in range(DEPTH):
        lp = {k: v[l] for k, v in small.items()}
        x, mk_p, mv_p, sp, ss = _decoder_layer(x, l, lp, big, lb_all[l], mem_p)
        mem_k.append(mk_p.reshape(BATCH, N_MEM, XA_H, XA_DH))
        mem_v.append(mv_p.reshape(BATCH, N_MEM, XA_H, XA_DH))
        st_p.append(sp)
        st_s.append(ss)
    y = _rmsnorm(x, norm_final_w, F32, TM // 2)
    y_prompt = y[:N_PROMPT].reshape(BATCH, SEQ, D_MODEL)
    y_sample = y[N_PROMPT:].reshape(DEC_BATCH, DEC_SEQ, D_MODEL)
    outs_p = [jnp.stack(s) for s in zip(*st_p)]
    outs_s = [jnp.stack(s) for s in zip(*st_s)]
    return (y_prompt, y_sample, jnp.stack(mem_k), jnp.stack(mem_v), *outs_p, *outs_s)
```
